```python
import math
import jax, jax.numpy as jnp
from jax import lax
import numpy as np

D_MODEL = 2048
BATCH = 4
SEQ = 2048
DEPTH = 4
DEC_BATCH = 128
DEC_SEQ = 1
PAST_LEN = 8192
PAGE_SIZE = 128

N_EVEN = (DEPTH + 1) // 2
N_ODD = DEPTH // 2
N_HEADS = 8
DIFF_DH = 64
DSA_DH = 128
IDX_HEADS = 8
IDX_DH = 64
DSA_TOPK = 256
MOBA_DH = 64
MOBA_BLOCK = 256
MOBA_TOPK = 3
MLA_Q_RANK = 512
MLA_KV_RANK = 256
MLA_NOPE = 128
MLA_ROPE = 64
MLA_VDH = 128
ROPE_THETA = 10000.0
REL_BUCKETS = 32
REL_MAX_DIST = 128
FFN_DIM = 5632
N_EXPERTS = 8
MOE_TOPK = 2
EXPERT_DIM = 7168
PLE_DIM = 256
DEEPNORM_ALPHA = (2 * DEPTH) ** 0.25
DEEPNORM_BETA = (8 * DEPTH) ** -0.25

QBLOCK = 128
MOBA_QBLOCK = 32
NEG_INF = -1e30
LN_EPS = 1e-5

EVEN_IN_SIZES = (N_HEADS * 2 * DIFF_DH, 2 * DIFF_DH, 2 * DIFF_DH,
                 N_HEADS * DSA_DH, DSA_DH, DSA_DH,
                 IDX_HEADS * IDX_DH, IDX_DH, IDX_HEADS)
ODD_IN_SIZES = (N_HEADS * MOBA_DH, MOBA_DH, MOBA_DH, MLA_Q_RANK, MLA_KV_RANK, MLA_ROPE)
EVEN_OUT = N_HEADS * 2 * DIFF_DH + N_HEADS * DSA_DH
ODD_OUT = N_HEADS * MOBA_DH + N_HEADS * MLA_VDH
DIFF_W = 4 * DIFF_DH
DSA_KV_W = 2 * DSA_DH
MOBA_KV_W = 2 * MOBA_DH
MLA_W = MLA_KV_RANK + MLA_ROPE

kernel_name = 'hybrid_diff_dsa_moba_mla_decoder_step'


def _split(h, sizes):
    out, o = [], 0
    for s in sizes:
        out.append(h[..., o:o + s])
        o += s
    return out


def _layer_norm(x, g, b):
    xf = x.astype(jnp.float32)
    mu = xf.mean(-1, keepdims=True)
    var = jnp.square(xf - mu).mean(-1, keepdims=True)
    return ((xf - mu) * lax.rsqrt(var + LN_EPS)).astype(x.dtype) * g + b


def _rms_norm(x, g):
    xf = x.astype(jnp.float32)
    return (xf * lax.rsqrt(jnp.square(xf).mean(-1, keepdims=True) + LN_EPS)).astype(x.dtype) * g


def _t5_bucket(dist):
    n = jnp.maximum(dist, 0)
    max_exact = REL_BUCKETS // 2
    nf = jnp.maximum(n, 1).astype(jnp.float32)
    large = max_exact + (jnp.log(nf / max_exact) / math.log(REL_MAX_DIST / max_exact)
                         * (REL_BUCKETS - max_exact)).astype(jnp.int32)
    large = jnp.minimum(large, REL_BUCKETS - 1)
    return jnp.where(n < max_exact, n, large)


def _rope(x, pos):
    half = x.shape[-1] // 2
    freq = jnp.power(ROPE_THETA, -jnp.arange(half, dtype=jnp.float32) / half)
    ang = pos.astype(jnp.float32)[..., None] * freq
    cos, sin = jnp.cos(ang).astype(x.dtype), jnp.sin(ang).astype(x.dtype)
    x1, x2 = x[..., :half], x[..., half:]
    return jnp.concatenate([x1 * cos - x2 * sin, x1 * sin + x2 * cos], -1)


def _map_qblocks(fn, block, *arrays):
    lq = arrays[0].shape[1]
    if lq <= block or lq % block:
        return fn(*arrays)
    nb = lq // block
    parts = tuple(jnp.moveaxis(a.reshape(a.shape[0], nb, block, *a.shape[2:]), 1, 0) for a in arrays)
    out = lax.map(lambda args: fn(*args), parts)
    out = jnp.moveaxis(out, 0, 1)
    return out.reshape(out.shape[0], lq, *out.shape[3:])


def _take_rows(rows, pos):
    return jax.vmap(lambda r, p: r[p])(rows, pos)


def _read_paged(pool, layer, page_table, lo, hi):
    rows = pool[page_table, layer, :, lo:hi]
    return rows.reshape(page_table.shape[0], -1, hi - lo)


def _take_paged(pool, layer, page_table, new_rows, pos):
    past = page_table.shape[1] * PAGE_SIZE
    pc = jnp.minimum(pos, past - 1)
    phys = jax.vmap(lambda pt, p: pt[p // PAGE_SIZE])(page_table, pc)
    cached = pool[phys, layer, pc % PAGE_SIZE]
    fresh = _take_rows(new_rows, jnp.clip(pos - past, 0, new_rows.shape[1] - 1))
    return jnp.where((pos < past)[..., None], cached, fresh)


def _rows_all(pool, layer, page_table, new_rows, lo, hi):
    fresh = new_rows[..., lo:hi]
    if page_table is None:
        return fresh
    return jnp.concatenate([_read_paged(pool, layer, page_table, lo, hi), fresh], axis=1)


def _rows_at(pool, layer, page_table, new_rows, pos):
    if page_table is None:
        return _take_rows(new_rows, pos)
    return _take_paged(pool, layer, page_table, new_rows, pos)


def _diff_attention(q, k, v, q_pos, lam, lam_init, subln_g, rel_bias):
    k_pos = jnp.arange(k.shape[1])
    scale = DIFF_DH ** -0.5

    def block(qb, qp):
        s = jnp.einsum('bqhcd,bkcd->bchqk', qb, k).astype(jnp.float32) * scale
        dist = qp[:, :, None] - k_pos
        bias = jnp.moveaxis(rel_bias[_t5_bucket(dist)], -1, 1).astype(jnp.float32)
        s = jnp.where((dist >= 0)[:, None, None], s + bias[:, None], NEG_INF)
        p = jax.nn.softmax(s, axis=-1)
        a = (p[:, 0] - lam * p[:, 1]).astype(v.dtype)
        return jnp.einsum('bhqk,bke->bqhe', a, v)

    o = _map_qblocks(block, QBLOCK, q, q_pos)
    o = _rms_norm(o, subln_g) * (1.0 - lam_init)
    return o.reshape(o.shape[0], o.shape[1], -1)


def _dsa_attention(q, iq, iw, q_pos, k_idx, kv_at, rel_bias):
    lk = k_idx.shape[1]
    topk = min(DSA_TOPK, lk // 4)
    k_pos = jnp.arange(lk)

    def block(qb, iqb, iwb, qp):
        sc = jax.nn.relu(jnp.einsum('bqhd,bkd->bqhk', iqb, k_idx).astype(jnp.float32) * IDX_DH ** -0.5)
        score = jnp.einsum('bqhk,bqh->bqk', sc, iwb.astype(jnp.float32) * IDX_HEADS ** -0.5)
        dist = qp[:, :, None] - k_pos
        score = jnp.where(dist >= 0, score, NEG_INF)
        _, sel = lax.top_k(score, topk)
        rows = kv_at(sel)
        ks, vs = rows[..., :DSA_DH], rows[..., DSA_DH:]
        s = jnp.einsum('bqhd,bqkd->bqhk', qb, ks).astype(jnp.float32) * DSA_DH ** -0.5
        dsel = qp[:, :, None] - sel
        bias = jnp.moveaxis(rel_bias[_t5_bucket(dsel)], -1, 2).astype(jnp.float32)
        s = jnp.where((dsel >= 0)[:, :, None], s + bias, NEG_INF)
        p = jax.nn.softmax(s, axis=-1).astype(vs.dtype)
        return jnp.einsum('bqhk,bqkd->bqhd', p, vs)

    o = _map_qblocks(block, QBLOCK, q, iq, iw, q_pos)
    return o.reshape(o.shape[0], o.shape[1], -1)


def _moba_attention(q, k_all, kv_at, q_pos, rel_bias):
    b, lk = k_all.shape[0], k_all.shape[1]
    n_full = lk // MOBA_BLOCK
    n_sel = min(MOBA_TOPK, n_full)
    scale = MOBA_DH ** -0.5
    offs = jnp.arange(MOBA_BLOCK)
    heads = jnp.arange(N_HEADS)
    if n_sel:
        means = k_all[:, :n_full * MOBA_BLOCK].astype(jnp.float32).reshape(
            b, n_full, MOBA_BLOCK, MOBA_DH).mean(2)

    def block(qb, qp):
        nq = qb.shape[1]
        own = qp // MOBA_BLOCK
        own_pos = jnp.broadcast_to(own[..., None] * MOBA_BLOCK + offs, (b, nq, MOBA_BLOCK))
        own_rows = kv_at(jnp.minimum(own_pos, lk - 1))
        d_own = qp[:, :, None] - own_pos
        s_own = jnp.einsum('bqhd,bqkd->bqhk', qb, own_rows[..., :MOBA_DH]).astype(jnp.float32) * scale
        b_own = jnp.moveaxis(rel_bias[_t5_bucket(d_own)], -1, 2).astype(jnp.float32)
        s_own = jnp.where((d_own >= 0)[:, :, None], s_own + b_own, NEG_INF)
        if not n_sel:
            p = jax.nn.softmax(s_own, axis=-1).astype(own_rows.dtype)
            return jnp.einsum('bqhk,bqkd->bqhd', p, own_rows[..., MOBA_DH:])
        gate = jnp.einsum('bqhd,bnd->bqhn', qb.astype(jnp.float32), means)
        gate = jnp.where(jnp.arange(n_full) < own[:, :, None, None], gate, NEG_INF)
        _, blk = lax.top_k(gate, n_sel)
        sel_ok = blk < own[:, :, None, None]
        sel_pos = blk[..., None] * MOBA_BLOCK + offs
        sel_rows = kv_at(sel_pos)
        d_sel = qp[:, :, None, None, None] - sel_pos
        s_sel = jnp.einsum('bqhd,bqhnkd->bqhnk', qb, sel_rows[..., :MOBA_DH]).astype(jnp.float32) * scale
        s_sel = s_sel + rel_bias[_t5_bucket(d_sel), heads[:, None, None]].astype(jnp.float32)
        s_sel = jnp.where(sel_ok[..., None], s_sel, NEG_INF).reshape(b, nq, N_HEADS, n_sel * MOBA_BLOCK)
        p = jax.nn.softmax(jnp.concatenate([s_sel, s_own], -1), axis=-1).astype(own_rows.dtype)
        p_sel = p[..., :n_sel * MOBA_BLOCK].reshape(b, nq, N_HEADS, n_sel, MOBA_BLOCK)
        p_own = p[..., n_sel * MOBA_BLOCK:]
        return (jnp.einsum('bqhnk,bqhnkd->bqhd', p_sel, sel_rows[..., MOBA_DH:])
                + jnp.einsum('bqhk,bqkd->bqhd', p_own, own_rows[..., MOBA_DH:]))

    o = _map_qblocks(block, MOBA_QBLOCK, q, q_pos)
    return o.reshape(o.shape[0], o.shape[1], -1)


def _mla_attention(q_lat, q_rope, c_kv, k_rope, q_pos):
    k_pos = jnp.arange(c_kv.shape[1])
    scale = (MLA_NOPE + MLA_ROPE) ** -0.5

    def block(qlb, qrb, qp):
        s = (jnp.einsum('bqhc,bkc->bqhk', qlb, c_kv)
             + jnp.einsum('bqhr,bkr->bqhk', qrb, k_rope)).astype(jnp.float32) * scale
        s = jnp.where((qp[:, :, None] - k_pos >= 0)[:, :, None], s, NEG_INF)
        p = jax.nn.softmax(s, axis=-1).astype(c_kv.dtype)
        return jnp.einsum('bqhk,bkc->bqhc', p, c_kv)

    return _map_qblocks(block, QBLOCK, q_lat, q_rope, q_pos)


def _even_mixer(x, pos, li, lam_init, w_in, diff_lambda, subln_g, w_out, rel_bias, pools, page_table):
    b, l, _ = x.shape
    dq, dk, dv, sq, sk, sv, iq, ik, iw = _split(x @ w_in, EVEN_IN_SIZES)
    diff_rows = jnp.concatenate([dk, dv], -1)
    dsa_rows = jnp.concatenate([sk, sv], -1)
    idx_rows = ik
    pool_diff, pool_dsa, pool_idx = pools
    lamf = diff_lambda.astype(jnp.float32)
    lam = jnp.exp(jnp.sum(lamf[0] * lamf[1])) - jnp.exp(jnp.sum(lamf[2] * lamf[3])) + lam_init
    diff_all = _rows_all(pool_diff, li, page_table, diff_rows, 0, DIFF_W)
    a = _diff_attention(dq.reshape(b, l, N_HEADS, 2, DIFF_DH),
                        diff_all[..., :2 * DIFF_DH].reshape(b, -1, 2, DIFF_DH),
                        diff_all[..., 2 * DIFF_DH:], pos, lam, lam_init, subln_g, rel_bias)
    k_idx_all = _rows_all(pool_idx, li, page_table, idx_rows, 0, IDX_DH)
    kv_at = lambda p: _rows_at(pool_dsa, li, page_table, dsa_rows, p)
    s = _dsa_attention(sq.reshape(b, l, N_HEADS, DSA_DH), iq.reshape(b, l, IDX_HEADS, IDX_DH), iw,
                       pos, k_idx_all, kv_at, rel_bias)
    out = jnp.concatenate([a, s], -1) @ w_out
    return out, diff_rows, dsa_rows, idx_rows


def _odd_mixer(x, pos, li, w_in, q_norm_g, kv_norm_g, w_q_up, w_uk, w_uv, w_out, rel_bias, pools, page_table):
    b, l, _ = x.shape
    mq, mk, mv, qa, kva, kr = _split(x @ w_in, ODD_IN_SIZES)
    moba_rows = jnp.concatenate([mk, mv], -1)
    c_kv = _rms_norm(kva, kv_norm_g)
    mla_rows = jnp.concatenate([c_kv, _rope(kr, pos)], -1)
    pool_moba, pool_mla = pools
    k_all = _rows_all(pool_moba, li, page_table, moba_rows, 0, MOBA_DH)
    kv_at = lambda p: _rows_at(pool_moba, li, page_table, moba_rows, p)
    m = _moba_attention(mq.reshape(b, l, N_HEADS, MOBA_DH), k_all, kv_at, pos, rel_bias)
    qf = (_rms_norm(qa, q_norm_g) @ w_q_up).reshape(b, l, N_HEADS, MLA_NOPE + MLA_ROPE)
    q_rope = _rope(qf[..., MLA_NOPE:], pos[..., None])
    q_lat = jnp.einsum('bqhd,chd->bqhc', qf[..., :MLA_NOPE], w_uk)
    mla_all = _rows_all(pool_mla, li, page_table, mla_rows, 0, MLA_W)
    o_lat = _mla_attention(q_lat, q_rope, mla_all[..., :MLA_KV_RANK], mla_all[..., MLA_KV_RANK:], pos)
    d = jnp.einsum('bqhc,chd->bqhd', o_lat, w_uv).reshape(b, l, -1)
    out = jnp.concatenate([m, d], -1) @ w_out
    return out, moba_rows, mla_rows


def _swiglu(x, w_in, w_out):
    g, u = jnp.split(x @ w_in, 2, axis=-1)
    return (jax.nn.silu(g) * u) @ w_out


def _moe(x, router, w_in, w_out):
    b, l, d = x.shape
    xf = x.reshape(b * l, d)
    logits = (xf @ router).astype(jnp.float32)
    top_v, top_i = lax.top_k(logits, MOE_TOPK)
    gates = jax.nn.softmax(top_v, axis=-1)
    combine = jnp.einsum('nk,nke->ne', gates, jax.nn.one_hot(top_i, N_EXPERTS, dtype=jnp.float32)).astype(x.dtype)
    y = jnp.zeros_like(xf)
    for e in range(N_EXPERTS):
        y = y + combine[:, e:e + 1] * _swiglu(xf, w_in[e], w_out[e])
    return y.reshape(b, l, d)


def _trunk(x, ple, pos, pools_even, pools_odd, page_table, w):
    diff_l, dsa_l, idx_l, moba_l, mla_l = [], [], [], [], []
    for i in range(DEPTH):
        j = i // 2
        if i % 2 == 0:
            lam_init = 0.8 - 0.6 * math.exp(-0.3 * i)
            h, r_diff, r_dsa, r_idx = _even_mixer(x, pos, j, lam_init, w['w_in_even'][j], w['diff_lambda'][j],
                                                  w['diff_subln_g'][j], w['w_out_even'][j], w['rel_bias'],
                                                  pools_even, page_table)
            diff_l.append(r_diff); dsa_l.append(r_dsa); idx_l.append(r_idx)
        else:
            h, r_moba, r_mla = _odd_mixer(x, pos, j, w['w_in_odd'][j], w['mla_q_norm_g'][j], w['mla_kv_norm_g'][j],
                                          w['mla_w_q_up'][j], w['mla_w_uk'][j], w['mla_w_uv'][j],
                                          w['w_out_odd'][j], w['rel_bias'], pools_odd, page_table)
            moba_l.append(r_moba); mla_l.append(r_mla)
        x = _layer_norm(DEEPNORM_ALPHA * x + h, w['ln1_g'][i], w['ln1_b'][i])
        if i % 2 == 0:
            f = _swiglu(x, w['ffn_w_in'][j], w['ffn_w_out'][j])
        else:
            f = _moe(x, w['moe_router'][j], w['moe_w_in'][j], w['moe_w_out'][j])
        x = _layer_norm(DEEPNORM_ALPHA * x + f, w['ln2_g'][i], w['ln2_b'][i])
        gate = jax.nn.sigmoid((x @ w['ple_w_gate'][i]).astype(jnp.float32)).astype(x.dtype)
        x = x + gate * (ple[i] @ w['ple_w_proj'][i])
    rows = (jnp.stack(diff_l, 1), jnp.stack(dsa_l, 1), jnp.stack(idx_l, 1), jnp.stack(moba_l, 1), jnp.stack(mla_l, 1))
    return x, rows


def setup_inputs(seed: int = 0) -> dict:
    key = jax.random.key(seed)
    ks = iter(jax.random.split(key, 48))
    f32 = jnp.float32

    def nrm(shape, scale):
        return jax.random.normal(next(ks), shape, f32) * scale

    def gain(shape):
        return 1.0 + 0.02 * jax.random.normal(next(ks), shape, f32)

    n_pages = PAST_LEN // PAGE_SIZE
    n_used = DEC_BATCH * n_pages
    n_phys = (n_used * 5 + 3) // 4
    x_prompt = nrm((BATCH, SEQ, D_MODEL), 1.0)
    x_sample = nrm((DEC_BATCH, DEC_SEQ, D_MODEL), 1.0)
    cache_diff_kv = nrm((n_phys, N_EVEN, PAGE_SIZE, DIFF_W), 1.0)
    cache_dsa_kv = nrm((n_phys, N_EVEN, PAGE_SIZE, DSA_KV_W), 1.0)
    cache_dsa_idx = nrm((n_phys, N_EVEN, PAGE_SIZE, IDX_DH), 1.0)
    cache_moba_kv = nrm((n_phys, N_ODD, PAGE_SIZE, MOBA_KV_W), 1.0)
    cache_mla = nrm((n_phys, N_ODD, PAGE_SIZE, MLA_W), 1.0)
    page_table = jax.random.permutation(next(ks), n_phys)[:n_used].reshape(DEC_BATCH, n_pages).astype(jnp.int32)
    p_prompt = nrm((DEPTH, BATCH, SEQ, PLE_DIM), 1.0)
    p_sample = nrm((DEPTH, DEC_BATCH, DEC_SEQ, PLE_DIM), 1.0)
    e_in = sum(EVEN_IN_SIZES)
    o_in = sum(ODD_IN_SIZES)
    return {
        'x_prompt': x_prompt,
        'x_sample': x_sample,
        'cache_diff_kv': cache_diff_kv,
        'cache_dsa_kv': cache_dsa_kv,
        'cache_dsa_idx': cache_dsa_idx,
        'cache_moba_kv': cache_moba_kv,
        'cache_mla': cache_mla,
        'page_table': page_table,
        'p_prompt': p_prompt,
        'p_sample': p_sample,
        'rel_bias': nrm((REL_BUCKETS, N_HEADS), 0.1),
        'w_in_even': nrm((N_EVEN, D_MODEL, e_in), D_MODEL ** -0.5),
        'diff_lambda': nrm((N_EVEN, 4, DIFF_DH), 0.1),
        'diff_subln_g': gain((N_EVEN, 2 * DIFF_DH)),
        'w_out_even': nrm((N_EVEN, EVEN_OUT, D_MODEL), EVEN_OUT ** -0.5 * DEEPNORM_BETA),
        'ffn_w_in': nrm((N_EVEN, D_MODEL, 2 * FFN_DIM), D_MODEL ** -0.5),
        'ffn_w_out': nrm((N_EVEN, FFN_DIM, D_MODEL), FFN_DIM ** -0.5 * DEEPNORM_BETA),
        'w_in_odd': nrm((N_ODD, D_MODEL, o_in), D_MODEL ** -0.5),
        'mla_q_norm_g': gain((N_ODD, MLA_Q_RANK)),
        'mla_kv_norm_g': gain((N_ODD, MLA_KV_RANK)),
        'mla_w_q_up': nrm((N_ODD, MLA_Q_RANK, N_HEADS * (MLA_NOPE + MLA_ROPE)), MLA_Q_RANK ** -0.5),
        'mla_w_uk': nrm((N_ODD, MLA_KV_RANK, N_HEADS, MLA_NOPE), MLA_KV_RANK ** -0.5),
        'mla_w_uv': nrm((N_ODD, MLA_KV_RANK, N_HEADS, MLA_VDH), MLA_KV_RANK ** -0.5),
        'w_out_odd': nrm((N_ODD, ODD_OUT, D_MODEL), ODD_OUT ** -0.5 * DEEPNORM_BETA),
        'moe_router': nrm((N_ODD, D_MODEL, N_EXPERTS), D_MODEL ** -0.5),
        'moe_w_in': nrm((N_ODD, N_EXPERTS, D_MODEL, 2 * EXPERT_DIM), D_MODEL ** -0.5),
        'moe_w_out': nrm((N_ODD, N_EXPERTS, EXPERT_DIM, D_MODEL), EXPERT_DIM ** -0.5 * DEEPNORM_BETA),
        'ln1_g': gain((DEPTH, D_MODEL)),
        'ln1_b': nrm((DEPTH, D_MODEL), 0.02),
        'ln2_g': gain((DEPTH, D_MODEL)),
        'ln2_b': nrm((DEPTH, D_MODEL), 0.02),
        'ple_w_gate': nrm((DEPTH, D_MODEL, D_MODEL), D_MODEL ** -0.5),
        'ple_w_proj': nrm((DEPTH, PLE_DIM, D_MODEL), PLE_DIM ** -0.5),
    }


def reference(x_prompt, x_sample, cache_diff_kv, cache_dsa_kv, cache_dsa_idx, cache_moba_kv, cache_mla,
              page_table, p_prompt, p_sample, rel_bias, w_in_even, diff_lambda, diff_subln_g, w_out_even,
              ffn_w_in, ffn_w_out, w_in_odd, mla_q_norm_g, mla_kv_norm_g, mla_w_q_up, mla_w_uk, mla_w_uv,
              w_out_odd, moe_router, moe_w_in, moe_w_out, ln1_g, ln1_b, ln2_g, ln2_b, ple_w_gate, ple_w_proj):
    w = dict(rel_bias=rel_bias, w_in_even=w_in_even, diff_lambda=diff_lambda, diff_subln_g=diff_subln_g,
             w_out_even=w_out_even, ffn_w_in=ffn_w_in, ffn_w_out=ffn_w_out, w_in_odd=w_in_odd,
             mla_q_norm_g=mla_q_norm_g, mla_kv_norm_g=mla_kv_norm_g, mla_w_q_up=mla_w_q_up, mla_w_uk=mla_w_uk,
             mla_w_uv=mla_w_uv, w_out_odd=w_out_odd, moe_router=moe_router, moe_w_in=moe_w_in,
             moe_w_out=moe_w_out, ln1_g=ln1_g, ln1_b=ln1_b, ln2_g=ln2_g, ln2_b=ln2_b,
             ple_w_gate=ple_w_gate, ple_w_proj=ple_w_proj)
    pos_p = jnp.arange(x_prompt.shape[1])[None, :]
    y_prompt, rows_p = _trunk(x_prompt, p_prompt, pos_p, (None, None, None), (None, None), None, w)
    past = page_table.shape[1] * PAGE_SIZE
    pos_s = past + jnp.arange(x_sample.shape[1])[None, :]
    y_sample, rows_s = _trunk(x_sample, p_sample, pos_s, (cache_diff_kv, cache_dsa_kv, cache_dsa_idx),
                              (cache_moba_kv, cache_mla), page_table, w)
    return (y_prompt, y_sample, rows_p[0], rows_s[0], rows_p[1], rows_s[1], rows_p[2], rows_s[2],
            rows_p[3], rows_s[3], rows_p[4], rows_s[4])
```

```python
import functools
import math

import jax
import jax.numpy as jnp
from jax import lax
from jax.experimental import pallas as pl
from jax.experimental.pallas import tpu as pltpu

F32, BF16, I32 = jnp.float32, jnp.bfloat16, jnp.int32

N_HEADS = 8
DIFF_DH = 64
DSA_DH = 128
IDX_HEADS = 8
IDX_DH = 64
DSA_TOPK = 256
MOBA_DH = 64
MOBA_BLOCK = 256
MOBA_TOPK = 3
MLA_Q_RANK = 512
MLA_KV_RANK = 256
MLA_NOPE = 128
MLA_ROPE = 64
MLA_VDH = 128
ROPE_THETA = 10000.0
REL_BUCKETS = 32
REL_MAX_DIST = 128
N_EXPERTS = 8
PAGE = 128
NEG = -1e30
LN_EPS = 1e-5
INT_MIN = -(2 ** 31)

LANES = 128
TQ = 128
TK = 256
SROWS = 16
CHUNK_PAGES = 16
VMEM_LIMIT = 52 * 1024 * 1024


def _cp(sem, vmem=VMEM_LIMIT):
    return pltpu.CompilerParams(dimension_semantics=sem, vmem_limit_bytes=vmem)


def _tile(n, pref, mult):
    best = None
    for t in range(mult, min(n, pref) + 1, mult):
        if n % t == 0:
            best = t
    return best if best is not None else n


def _mm_kernel(x_ref, w_ref, o_ref):
    o_ref[...] = jnp.dot(x_ref[...], w_ref[...], preferred_element_type=F32).astype(o_ref.dtype)


def _matmul(x, w, out_dtype, name, tn_pref=512):
    m, k = x.shape
    n = w.shape[1]
    tm = _tile(m, 640, 16)
    tn = _tile(n, tn_pref, LANES)
    return pl.pallas_call(
        _mm_kernel, grid=(m // tm, n // tn),
        in_specs=[pl.BlockSpec((tm, k), lambda i, j: (i, 0)), pl.BlockSpec((k, tn), lambda i, j: (0, j))],
        out_specs=pl.BlockSpec((tm, tn), lambda i, j: (i, j)),
        out_shape=jax.ShapeDtypeStruct((m, n), out_dtype),
        compiler_params=_cp(("parallel", "arbitrary")), name=name)(x, w)


def _swiglu_kernel(x_ref, wg_ref, wu_ref, o_ref):
    x = x_ref[...]
    g = jnp.dot(x, wg_ref[...], preferred_element_type=F32)
    u = jnp.dot(x, wu_ref[...], preferred_element_type=F32)
    o_ref[...] = (g * jax.nn.sigmoid(g) * u).astype(o_ref.dtype)


def _swiglu(x, w_in, name):
    m, k = x.shape
    e, _, f2 = w_in.shape
    f = f2 // 2
    tm = _tile(m, 640, 16)
    tn = _tile(f, 512, LANES)
    nper = f // tn
    return pl.pallas_call(
        _swiglu_kernel, grid=(m // tm, e * nper),
        in_specs=[pl.BlockSpec((tm, k), lambda i, j: (i, 0)),
                  pl.BlockSpec((None, k, tn), lambda i, j: (j // nper, 0, j % nper)),
                  pl.BlockSpec((None, k, tn), lambda i, j: (j // nper, 0, j % nper + nper))],
        out_specs=pl.BlockSpec((tm, tn), lambda i, j: (i, j)),
        out_shape=jax.ShapeDtypeStruct((m, e * f), BF16),
        compiler_params=_cp(("parallel", "arbitrary")), name=name)(x, w_in, w_in)


def _mmk_kernel(h_ref, w_ref, o_ref, acc_ref):
    k = pl.program_id(2)

    @pl.when(k == 0)
    def _():
        acc_ref[...] = jnp.zeros_like(acc_ref)

    acc_ref[...] += jnp.dot(h_ref[...], w_ref[...], preferred_element_type=F32)

    @pl.when(k == pl.num_programs(2) - 1)
    def _():
        o_ref[...] = acc_ref[...]


def _mmk_gated_kernel(h_ref, w_ref, c_ref, o_ref, acc_ref, tot_ref, *, nper):
    k = pl.program_id(2)

    @pl.when(k == 0)
    def _():
        tot_ref[...] = jnp.zeros_like(tot_ref)

    @pl.when(k % nper == 0)
    def _():
        acc_ref[...] = jnp.zeros_like(acc_ref)

    acc_ref[...] += jnp.dot(h_ref[...], w_ref[...], preferred_element_type=F32)

    @pl.when(k % nper == nper - 1)
    def _():
        lane = lax.broadcasted_iota(I32, c_ref.shape, 1)
        gate = jnp.sum(jnp.where(lane == k // nper, c_ref[...], 0.0), axis=-1, keepdims=True)
        tot_ref[...] += gate * acc_ref[...]

    @pl.when(k == pl.num_programs(2) - 1)
    def _():
        o_ref[...] = tot_ref[...]


def _matmul_k(h, w, name, combine=None, n_groups=1):
    m, k = h.shape
    n = w.shape[1]
    tm = _tile(m, 640, 16)
    tn = _tile(n, 1024, LANES)
    tk = _tile(k // n_groups, 512, LANES)
    grid = (m // tm, n // tn, k // tk)
    h_spec = pl.BlockSpec((tm, tk), lambda i, j, kk: (i, kk))
    w_spec = pl.BlockSpec((tk, tn), lambda i, j, kk: (kk, j))
    o_spec = pl.BlockSpec((tm, tn), lambda i, j, kk: (i, j))
    out_shape = jax.ShapeDtypeStruct((m, n), F32)
    sem = ("parallel", "parallel", "arbitrary")
    if combine is None:
        return pl.pallas_call(_mmk_kernel, grid=grid, in_specs=[h_spec, w_spec], out_specs=o_spec,
                              out_shape=out_shape, scratch_shapes=[pltpu.VMEM((tm, tn), F32)],
                              compiler_params=_cp(sem), name=name)(h, w)
    kern = functools.partial(_mmk_gated_kernel, nper=(k // n_groups) // tk)
    c_spec = pl.BlockSpec((tm, LANES), lambda i, j, kk: (i, 0))
    return pl.pallas_call(kern, grid=grid, in_specs=[h_spec, w_spec, c_spec], out_specs=o_spec,
                          out_shape=out_shape,
                          scratch_shapes=[pltpu.VMEM((tm, tn), F32), pltpu.VMEM((tm, tn), F32)],
                          compiler_params=_cp(sem), name=name)(h, w, combine)


def _ln_kernel(x_ref, h_ref, g_ref, b_ref, o_ref, ob_ref, *, alpha):
    z = alpha * x_ref[...] + h_ref[...]
    mu = jnp.mean(z, axis=-1, keepdims=True)
    zc = z - mu
    var = jnp.mean(zc * zc, axis=-1, keepdims=True)
    y = zc * lax.rsqrt(var + LN_EPS) * g_ref[...] + b_ref[...]
    o_ref[...] = y
    ob_ref[...] = y.astype(BF16)


def _deepnorm_ln(x, h, g, b, alpha, name):
    m, d = x.shape
    tm = _tile(m, 320, 16)
    row = pl.BlockSpec((tm, d), lambda i: (i, 0))
    vec = pl.BlockSpec((1, d), lambda i: (0, 0))
    return pl.pallas_call(
        functools.partial(_ln_kernel, alpha=alpha), grid=(m // tm,),
        in_specs=[row, row, vec, vec], out_specs=[row, row],
        out_shape=[jax.ShapeDtypeStruct((m, d), F32), jax.ShapeDtypeStruct((m, d), BF16)],
        compiler_params=_cp(("parallel",)), name=name)(x, h, g.reshape(1, d), b.reshape(1, d))


def _ple_kernel(xb_ref, x_ref, wg_ref, pb_ref, wp_ref, o_ref, ob_ref):
    gate = jax.nn.sigmoid(jnp.dot(xb_ref[...], wg_ref[...], preferred_element_type=F32))
    proj = jnp.dot(pb_ref[...], wp_ref[...], preferred_element_type=F32)
    y = x_ref[...] + gate * proj
    o_ref[...] = y
    ob_ref[...] = y.astype(BF16)


def _ple(xb, x, wg, pb, wp, name):
    m, d = x.shape
    pd = pb.shape[1]
    tm = _tile(m, 640, 16)
    tn = _tile(d, 512, LANES)
    return pl.pallas_call(
        _ple_kernel, grid=(m // tm, d // tn),
        in_specs=[pl.BlockSpec((tm, d), lambda i, j: (i, 0)), pl.BlockSpec((tm, tn), lambda i, j: (i, j)),
                  pl.BlockSpec((d, tn), lambda i, j: (0, j)), pl.BlockSpec((tm, pd), lambda i, j: (i, 0)),
                  pl.BlockSpec((pd, tn), lambda i, j: (0, j))],
        out_specs=[pl.BlockSpec((tm, tn), lambda i, j: (i, j)), pl.BlockSpec((tm, tn), lambda i, j: (i, j))],
        out_shape=[jax.ShapeDtypeStruct((m, d), F32), jax.ShapeDtypeStruct((m, d), BF16)],
        compiler_params=_cp(("parallel", "arbitrary")), name=name)(xb, x, wg, pb, wp)


def _router_kernel(xb_ref, w_ref, o_ref):
    logits = jnp.dot(xb_ref[...], w_ref[...], preferred_element_type=F32)
    lane = lax.broadcasted_iota(I32, logits.shape, 1)
    logits = jnp.where(lane < N_EXPERTS, logits, -jnp.inf)
    m1 = jnp.max(logits, axis=-1, keepdims=True)
    i1 = jnp.min(jnp.where(logits == m1, lane, LANES), axis=-1, keepdims=True)
    rest = jnp.where(lane == i1, -jnp.inf, logits)
    m2 = jnp.max(rest, axis=-1, keepdims=True)
    i2 = jnp.min(jnp.where(rest == m2, lane, LANES), axis=-1, keepdims=True)
    e2 = jnp.exp(m2 - m1)
    den = 1.0 + e2
    o_ref[...] = jnp.where(lane == i1, 1.0 / den, 0.0) + jnp.where(lane == i2, e2 / den, 0.0)


def _router(xb, w, name):
    m, d = xb.shape
    tm = _tile(m, 640, 16)
    return pl.pallas_call(
        _router_kernel, grid=(m // tm,),
        in_specs=[pl.BlockSpec((tm, d), lambda i: (i, 0)), pl.BlockSpec((d, LANES), lambda i: (0, 0))],
        out_specs=pl.BlockSpec((tm, LANES), lambda i: (i, 0)),
        out_shape=jax.ShapeDtypeStruct((m, LANES), F32),
        compiler_params=_cp(("parallel",)), name=name)(xb, w)


def _t5_bucket(d):
    n = jnp.maximum(d, 0)
    max_exact = REL_BUCKETS // 2
    nf = jnp.maximum(n, 1).astype(F32)
    large = max_exact + (jnp.log(nf / max_exact) / math.log(REL_MAX_DIST / max_exact)
                         * (REL_BUCKETS - max_exact)).astype(I32)
    large = jnp.minimum(large, REL_BUCKETS - 1)
    return jnp.where(n < max_exact, n, large)


def _bias_rows(bucket, rb_ref, h):
    acc = jnp.zeros(bucket.shape, F32)
    for b in range(REL_BUCKETS):
        acc = jnp.where(bucket == b, rb_ref[b, h], acc)
    return acc


def _tbl_prompt_kernel(rb_ref, o_ref, *, hb):
    t = pl.program_id(0)
    off = (t // 2) * TK + (t % 2) * TQ
    r = lax.broadcasted_iota(I32, (TQ, TK), 0)
    c = lax.broadcasted_iota(I32, (TQ, TK), 1)
    d = off + r - c
    bucket = _t5_bucket(d)
    for h in range(hb):
        o_ref[0, h * TQ:(h + 1) * TQ, :] = jnp.where(d >= 0, _bias_rows(bucket, rb_ref, h), NEG)


def _bias_tables_prompt(rb):
    hb = rb.shape[1]
    return pl.pallas_call(
        functools.partial(_tbl_prompt_kernel, hb=hb), grid=(6,),
        in_specs=[pl.BlockSpec(memory_space=pltpu.SMEM)],
        out_specs=pl.BlockSpec((1, hb * TQ, TK), lambda t: (t, 0, 0)),
        out_shape=jax.ShapeDtypeStruct((6, hb * TQ, TK), F32),
        compiler_params=_cp(("arbitrary",)), name="bias_tables_prompt")(rb)


def _tbl_sample_kernel(rb_ref, o_ref, *, past):
    k = lax.broadcasted_iota(I32, (1, o_ref.shape[1]), 1)
    bucket = _t5_bucket(past - k)
    for h in range(N_HEADS):
        row = _bias_rows(bucket, rb_ref, h)
        o_ref[h:h + 1, :] = row
        o_ref[N_HEADS + h:N_HEADS + h + 1, :] = row


def _bias_table_sample(rb, past):
    w = past + LANES
    return pl.pallas_call(
        functools.partial(_tbl_sample_kernel, past=past),
        in_specs=[pl.BlockSpec(memory_space=pltpu.SMEM)],
        out_specs=pl.BlockSpec(memory_space=pltpu.VMEM),
        out_shape=jax.ShapeDtypeStruct((SROWS, w), F32),
        compiler_params=_cp(None), name="bias_table_sample")(rb)


def _topk_additive(score, key_ref, k):
    r, n = score.shape
    score = jnp.where(score == 0.0, 0.0, score)
    bits = pltpu.bitcast(score, I32)
    key_ref[...] = jnp.where(bits < 0, bits ^ 0x7FFFFFFF, bits)
    kf = float(k)

    def count(cond):
        return jnp.sum(jnp.where(cond, 1.0, 0.0), axis=-1, keepdims=True)

    t0 = jnp.where(count(key_ref[...] >= 0) >= kf, 0, INT_MIN).astype(I32)

    def value_bit(it, t):
        cand = t + lax.shift_left(jnp.int32(1), 30 - it)
        return jnp.where(count(key_ref[...] >= cand) >= kf, cand, t)

    t = lax.fori_loop(0, 31, value_bit, t0)
    need = kf - count(key_ref[...] > t)
    idx_bits = max(1, (n - 1).bit_length())

    def index_bit(it, x):
        cand = x + lax.shift_left(jnp.int32(1), idx_bits - 1 - it)
        idx = lax.broadcasted_iota(I32, (r, n), 1)
        below = count(jnp.logical_and(key_ref[...] == t, idx < cand))
        return jnp.where(below < need, cand, x)

    x = lax.fori_loop(0, idx_bits, index_bit, jnp.zeros((r, 1), I32))
    key = key_ref[...]
    idx = lax.broadcasted_iota(I32, (r, n), 1)
    sel = jnp.logical_or(key > t, jnp.logical_and(key == t, idx <= x))
    return jnp.where(sel, 0.0, NEG)


def _idx_prompt_kernel(iq_ref, kx_ref, wx_ref, o_ref, key_ref, *, topk):
    i = pl.program_id(1)
    lk = kx_ref.shape[1]
    lane = lax.broadcasted_iota(I32, (1, LANES), 1)
    kx = jnp.where(lane < IDX_DH, kx_ref[0], 0.0)
    kk = (kx + pltpu.roll(kx, IDX_DH, 1)).astype(BF16)
    score = jnp.zeros((TQ, lk), F32)
    for h in range(IDX_HEADS):
        slab = iq_ref[0, :, (h // 2) * LANES:(h // 2 + 1) * LANES]
        keep = (lane < IDX_DH) if h % 2 == 0 else (lane >= IDX_DH)
        qh = jnp.where(keep, slab, 0.0).astype(BF16)
        sc = lax.dot_general(qh, kk, (((1,), (1,)), ((), ())), preferred_element_type=F32)
        sc = jnp.maximum(sc * IDX_DH ** -0.5, 0.0)
        wh = wx_ref[0, :, IDX_DH + h:IDX_DH + h + 1] * IDX_HEADS ** -0.5
        score = score + sc * wh
    qpos = i * TQ + lax.broadcasted_iota(I32, (TQ, lk), 0)
    kpos = lax.broadcasted_iota(I32, (TQ, lk), 1)
    score = jnp.where(kpos <= qpos, score, NEG)
    o_ref[0] = _topk_additive(score, key_ref, topk).astype(BF16)


def _idx_prompt(h3, topk):
    b, l, _ = h3.shape
    return pl.pallas_call(
        functools.partial(_idx_prompt_kernel, topk=topk), grid=(b, l // TQ),
        in_specs=[pl.BlockSpec((1, TQ, 512), lambda bi, i: (bi, i, 4)),
                  pl.BlockSpec((1, l, LANES), lambda bi, i: (bi, 0, 24)),
                  pl.BlockSpec((1, TQ, LANES), lambda bi, i: (bi, i, 24))],
        out_specs=pl.BlockSpec((1, TQ, l), lambda bi, i: (bi, i, 0)),
        out_shape=jax.ShapeDtypeStruct((b, l, l), BF16),
        scratch_shapes=[pltpu.VMEM((TQ, l), I32)],
        compiler_params=_cp(("parallel", "arbitrary")), name="dsa_index_prompt")(h3, h3, h3)


def _flash_kernel(*refs, variant, scale, n_sel, lam_init):
    if variant == "A":
        q_ref, k_ref, v_ref, tbl_ref, lam_ref, g_ref, o_ref, q_scr, s_scr, p_scr, m_scr, l_scr, acc_scr = refs
    elif variant == "B":
        q_ref, k_ref, v_ref, tbl_ref, mask_ref, o_ref, q_scr, s_scr, p_scr, m_scr, l_scr, acc_scr = refs
    elif variant == "C":
        q_ref, k_ref, tbl_ref, o_ref, q_scr, s_scr, p_scr, m_scr, l_scr, acc_scr, sel_scr, mean_scr = refs
        v_ref = k_ref
    else:
        q_ref, k_ref, tbl_ref, o_ref, q_scr, s_scr, p_scr, m_scr, l_scr, acc_scr = refs
        v_ref = k_ref
    i = pl.program_id(1)
    rows_total = q_scr.shape[0]
    groups = rows_total // TQ
    hb = tbl_ref.shape[1] // TQ
    dv = acc_scr.shape[1]
    own = (i * TQ) // TK
    parity = ((i * TQ) % TK) // TQ
    lane = lax.broadcasted_iota(I32, (TQ, LANES), 1)

    if variant == "A":
        for c in range(2):
            for h in range(N_HEADS):
                slab = q_ref[0, :, h * LANES:(h + 1) * LANES]
                keep = (lane < DIFF_DH) if c == 0 else (lane >= DIFF_DH)
                g = c * N_HEADS + h
                q_scr[g * TQ:(g + 1) * TQ, :] = jnp.where(keep, slab, 0.0).astype(BF16)
    elif variant in ("B", "C"):
        for h in range(N_HEADS):
            q_scr[h * TQ:(h + 1) * TQ, :] = q_ref[0, :, h * LANES:(h + 1) * LANES].astype(BF16)
    else:
        dq = q_scr.shape[1]
        for h in range(N_HEADS):
            q_scr[h * TQ:(h + 1) * TQ, :] = q_ref[0, :, h * dq:(h + 1) * dq]
    m_scr[...] = jnp.full(m_scr.shape, NEG, F32)
    l_scr[...] = jnp.zeros(l_scr.shape, F32)
    acc_scr[...] = jnp.zeros(acc_scr.shape, F32)

    if variant == "C":
        nblk = k_ref.shape[1] // TK
        mean_scr[...] = jnp.zeros(mean_scr.shape, F32)
        for n in range(nblk):
            mean_scr[n:n + 1, :] = jnp.sum(k_ref[0, n * TK:(n + 1) * TK, :], axis=0, keepdims=True) * (1.0 / TK)
        nb_pad = mean_scr.shape[0]
        nlane = lax.broadcasted_iota(I32, (TQ, nb_pad), 1)
        for h in range(N_HEADS):
            gate = lax.dot_general(q_ref[0, :, h * LANES:(h + 1) * LANES], mean_scr[...],
                                   (((1,), (1,)), ((), ())), preferred_element_type=F32)
            rank = jnp.zeros((TQ, nb_pad), F32)
            for n2 in range(nblk):
                col = gate[:, n2:n2 + 1]
                beats = jnp.where(col > gate, 1.0,
                                  jnp.where(col == gate, jnp.where(n2 < nlane, 1.0, 0.0), 0.0))
                rank = rank + beats * (n2 < own).astype(F32)
            add = jnp.where(nlane < own, jnp.where(rank < n_sel, 0.0, NEG), 0.0)
            for n in range(nblk):
                sel_scr[n, h * TQ:(h + 1) * TQ, :] = add[:, n:n + 1]

    def step(j, carry):
        off = pl.multiple_of(j * TK, TK)
        kb = k_ref[0, pl.ds(off, TK), :].astype(BF16)
        s_scr[...] = lax.dot_general(q_scr[...], kb, (((1,), (1,)), ((), ())), preferred_element_type=F32)
        t = jnp.minimum(own - j, 2) * 2 + parity
        if variant == "B":
            extra = mask_ref[0, :, pl.ds(off, TK)].astype(F32)
        for g in range(groups):
            rows = slice(g * TQ, (g + 1) * TQ)
            hrow = (g % hb) * TQ
            s = s_scr[rows, :] * scale + tbl_ref[t, hrow:hrow + TQ, :]
            if variant == "B":
                s = s + extra
            if variant == "C":
                s = s + sel_scr[j, rows, :]
            m_prev = m_scr[rows, :]
            m_new = jnp.maximum(m_prev, jnp.max(s, axis=-1, keepdims=True))
            alpha = jnp.exp(m_prev - m_new)
            p = jnp.exp(s - m_new)
            l_scr[rows, :] = alpha * l_scr[rows, :] + jnp.sum(p, axis=-1, keepdims=True)
            m_scr[rows, :] = m_new
            p_scr[rows, :] = p.astype(BF16)
            acc_scr[rows, :] = acc_scr[rows, :] * alpha
        vb = v_ref[0, pl.ds(off, TK), 0:dv].astype(BF16)
        acc_scr[...] += jnp.dot(p_scr[...], vb, preferred_element_type=F32)
        return carry

    lax.fori_loop(0, own + 1, step, 0)

    def head_out(g):
        rows = slice(g * TQ, (g + 1) * TQ)
        return acc_scr[rows, :] / l_scr[rows, :]

    if variant == "A":
        lv = lam_ref[...]
        lam = (jnp.exp(jnp.sum(lv[0:1] * lv[1:2], axis=-1, keepdims=True))
               - jnp.exp(jnp.sum(lv[2:3] * lv[3:4], axis=-1, keepdims=True)) + lam_init)
        for h in range(N_HEADS):
            o = head_out(h) - lam * head_out(N_HEADS + h)
            o = o * lax.rsqrt(jnp.mean(o * o, axis=-1, keepdims=True) + LN_EPS)
            o_ref[0, :, h * LANES:(h + 1) * LANES] = (o * g_ref[...] * (1.0 - lam_init)).astype(o_ref.dtype)
    elif variant == "C":
        for pair in range(N_HEADS // 2):
            a0 = pltpu.roll(head_out(2 * pair), MOBA_DH, 1)
            a1 = head_out(2 * pair + 1)
            o_ref[0, :, pair * LANES:(pair + 1) * LANES] = jnp.where(lane < MOBA_DH, a0, a1).astype(o_ref.dtype)
    else:
        for h in range(N_HEADS):
            o_ref[0, :, h * dv:(h + 1) * dv] = head_out(h).astype(o_ref.dtype)


def _flash(variant, q3, kv3, tbl, *, q_blk, q_width, k_blk, k_width, v_blk=None, dv, out_width, groups,
           scale, mask=None, lam=None, subln_g=None, lam_init=0.0, n_sel=0, name):
    b, l, _ = kv3.shape
    rows = groups * TQ
    dq = k_width
    in_specs = [pl.BlockSpec((1, TQ, q_width), lambda bi, i: (bi, i, q_blk)),
                pl.BlockSpec((1, l, k_width), lambda bi, i: (bi, 0, k_blk))]
    args = [q3, kv3]
    if v_blk is not None:
        in_specs.append(pl.BlockSpec((1, l, dv), lambda bi, i: (bi, 0, v_blk)))
        args.append(kv3)
    in_specs.append(pl.BlockSpec(tbl.shape, lambda bi, i: (0, 0, 0)))
    args.append(tbl)
    if variant == "A":
        in_specs += [pl.BlockSpec(lam.shape, lambda bi, i: (0, 0)), pl.BlockSpec((1, LANES), lambda bi, i: (0, 0))]
        args += [lam, subln_g.reshape(1, LANES)]
    if variant == "B":
        in_specs.append(pl.BlockSpec((1, TQ, l), lambda bi, i: (bi, i, 0)))
        args.append(mask)
    scratch = [pltpu.VMEM((rows, dq), BF16), pltpu.VMEM((rows, TK), F32), pltpu.VMEM((rows, TK), BF16),
               pltpu.VMEM((rows, 1), F32), pltpu.VMEM((rows, 1), F32), pltpu.VMEM((rows, dv), F32)]
    if variant == "C":
        nblk = l // TK
        scratch += [pltpu.VMEM((nblk, rows, 1), F32), pltpu.VMEM((max(8, nblk), LANES), F32)]
    kern = functools.partial(_flash_kernel, variant=variant, scale=scale, n_sel=n_sel, lam_init=lam_init)
    return pl.pallas_call(
        kern, grid=(b, l // TQ), in_specs=in_specs,
        out_specs=pl.BlockSpec((1, TQ, out_width), lambda bi, i: (bi, i, 0)),
        out_shape=jax.ShapeDtypeStruct((b, l, out_width), BF16),
        scratch_shapes=scratch, compiler_params=_cp(("parallel", "arbitrary")), name=name)(*args)


def _page_copy(pool_ref, pt_ref, buf, sem, layer, seq, slot, page):
    return pltpu.make_async_copy(pool_ref.at[pt_ref[seq, page], layer], buf.at[slot, page], sem.at[slot])


def _fetch_pages(pool_ref, pt_ref, buf, sem, layer, n_pages):
    s = pl.program_id(0)
    slot = s % 2

    def start(seq, into):
        for p in range(n_pages):
            _page_copy(pool_ref, pt_ref, buf, sem, layer, seq, into, p).start()

    @pl.when(s == 0)
    def _():
        start(0, 0)

    @pl.when(s + 1 < pl.num_programs(0))
    def _():
        start(s + 1, 1 - slot)

    for p in range(n_pages):
        _page_copy(pool_ref, pt_ref, buf, sem, layer, s, slot, p).wait()
    return slot


def _sample_attn_kernel(*refs, variant, layer, pieces, v_lo, v_hi, scale, n_sel):
    if variant == "B":
        pt_ref, q_ref, pool_ref, new_ref, tb_ref, ex_ref, o_ref, buf, sem, s_scr, p_scr = refs
    elif variant == "C":
        pt_ref, q_ref, pool_ref, new_ref, tb_ref, o_ref, buf, sem, s_scr, p_scr, mean_scr, e_scr = refs
    else:
        pt_ref, q_ref, pool_ref, new_ref, tb_ref, o_ref, buf, sem, s_scr, p_scr = refs
    n_pages = buf.shape[1]
    past = n_pages * PAGE
    ckeys = CHUNK_PAGES * PAGE
    n_chunks = n_pages // CHUNK_PAGES
    slot = _fetch_pages(pool_ref, pt_ref, buf, sem, layer, n_pages)

    qf = q_ref[0]
    qb = qf.astype(BF16)

    if variant == "C":
        nblk = past // MOBA_BLOCK

        @pl.when(pl.program_id(0) == 0)
        def _():
            blk = lax.broadcasted_iota(I32, e_scr.shape, 0)
            key = lax.broadcasted_iota(I32, e_scr.shape, 1)
            e_scr[...] = jnp.where(key // MOBA_BLOCK == blk, 1.0, 0.0).astype(BF16)

    for c in range(n_chunks):
        kc = buf[slot, c * CHUNK_PAGES:(c + 1) * CHUNK_PAGES].reshape(ckeys, buf.shape[3])
        s = None
        for lo, hi in pieces:
            part = lax.dot_general(qb[:, lo:hi], kc[:, lo:hi].astype(BF16), (((1,), (1,)), ((), ())),
                                   preferred_element_type=F32)
            s = part if s is None else s + part
        s_scr[:, c * ckeys:(c + 1) * ckeys] = s
        if variant == "C":
            per = ckeys // MOBA_BLOCK
            for n in range(per):
                mean_scr[c * per + n:c * per + n + 1, :] = (
                    jnp.sum(kc[n * MOBA_BLOCK:(n + 1) * MOBA_BLOCK, :], axis=0, keepdims=True) * (1.0 / MOBA_BLOCK))

    s = s_scr[...] * scale
    new = new_ref[0]
    s_new = None
    for lo, hi in pieces:
        part = jnp.sum(qf[:, lo:hi] * new[:, lo:hi], axis=-1, keepdims=True)
        s_new = part if s_new is None else s_new + part
    s_new = s_new * scale
    if tb_ref is not None:
        s = s + tb_ref[:, 0:past]
        s_new = s_new + tb_ref[:, past:past + 1]
    if variant == "B":
        s = s + ex_ref[0, :, 0:past]
        s_new = s_new + ex_ref[0, :, past:past + 1]
    if variant == "C":
        gate = lax.dot_general(qf, mean_scr[...], (((1,), (1,)), ((), ())), preferred_element_type=F32)
        nlane = lax.broadcasted_iota(I32, gate.shape, 1)
        rank = jnp.zeros(gate.shape, F32)
        for n2 in range(nblk):
            col = gate[:, n2:n2 + 1]
            rank = rank + jnp.where(col > gate, 1.0,
                                    jnp.where(col == gate, jnp.where(n2 < nlane, 1.0, 0.0), 0.0))
        chosen = jnp.where(rank < n_sel, 1.0, 0.0).astype(BF16)
        keep = jnp.dot(chosen, e_scr[...], preferred_element_type=F32)
        s = s + (keep - 1.0) * (-NEG)

    m = jnp.maximum(jnp.max(s, axis=-1, keepdims=True), s_new)
    p = jnp.exp(s - m)
    p_new = jnp.exp(s_new - m)
    denom = jnp.sum(p, axis=-1, keepdims=True) + p_new
    p_scr[...] = p.astype(BF16)
    acc = p_new * new[:, v_lo:v_hi]
    for c in range(n_chunks):
        vc = buf[slot, c * CHUNK_PAGES:(c + 1) * CHUNK_PAGES].reshape(ckeys, buf.shape[3])[:, v_lo:v_hi]
        acc = acc + jnp.dot(p_scr[:, c * ckeys:(c + 1) * ckeys], vc.astype(BF16), preferred_element_type=F32)
    o_ref[0] = acc / denom


def _sample_attn(variant, q, pool, layer, page_table, new_rows, tb, *, pieces, v_lo, v_hi, scale,
                 extra=None, n_sel=0, name):
    ns, _, dq = q.shape
    n_pages = page_table.shape[1]
    past = n_pages * PAGE
    w = pool.shape[3]
    in_specs = [pl.BlockSpec((1, SROWS, dq), lambda s, pt: (s, 0, 0)),
                pl.BlockSpec(memory_space=pl.ANY),
                pl.BlockSpec((1, 1, w), lambda s, pt: (s, 0, 0))]
    args = [q, pool, new_rows]
    if tb is not None:
        in_specs.append(pl.BlockSpec(tb.shape, lambda s, pt: (0, 0)))
        args.append(tb)
    if variant == "B":
        in_specs.append(pl.BlockSpec((1, 1, past + LANES), lambda s, pt: (s, 0, 0)))
        args.append(extra)
    scratch = [pltpu.VMEM((2, n_pages, PAGE, w), F32), pltpu.SemaphoreType.DMA((2,)),
               pltpu.VMEM((SROWS, past), F32), pltpu.VMEM((SROWS, past), BF16)]
    if variant == "C":
        nblk = past // MOBA_BLOCK
        scratch += [pltpu.VMEM((nblk, w), F32), pltpu.VMEM((nblk, past), BF16)]
    kern = functools.partial(_sample_attn_kernel, variant=variant, layer=layer, pieces=pieces,
                             v_lo=v_lo, v_hi=v_hi, scale=scale, n_sel=n_sel)
    if tb is None:
        inner = kern

        def kern(pt_ref, q_ref, pool_ref, new_ref, *rest):
            return inner(pt_ref, q_ref, pool_ref, new_ref, None, *rest)

    return pl.pallas_call(
        kern,
        grid_spec=pltpu.PrefetchScalarGridSpec(
            num_scalar_prefetch=1, grid=(ns,), in_specs=in_specs,
            out_specs=pl.BlockSpec((1, SROWS, v_hi - v_lo), lambda s, pt: (s, 0, 0)),
            scratch_shapes=scratch),
        out_shape=jax.ShapeDtypeStruct((ns, SROWS, v_hi - v_lo), F32),
        compiler_params=_cp(("arbitrary",)), name=name)(page_table, *args)


def _idx_sample_kernel(pt_ref, iq_ref, w_ref, pool_ref, new_ref, o_ref, buf, sem, row_scr, key_scr, *,
                       layer, topk):
    n_pages = buf.shape[1]
    past = n_pages * PAGE
    ckeys = CHUNK_PAGES * PAGE
    slot = _fetch_pages(pool_ref, pt_ref, buf, sem, layer, n_pages)
    iqf = iq_ref[0]
    iqb = iqf.astype(BF16)
    wh = w_ref[0][:, 0:1] * IDX_HEADS ** -0.5
    for c in range(n_pages // CHUNK_PAGES):
        kc = buf[slot, c * CHUNK_PAGES:(c + 1) * CHUNK_PAGES].reshape(ckeys, buf.shape[3])
        sc = lax.dot_general(iqb, kc.astype(BF16), (((1,), (1,)), ((), ())), preferred_element_type=F32)
        sc = jnp.maximum(sc * IDX_DH ** -0.5, 0.0)
        row_scr[:, c * ckeys:(c + 1) * ckeys] = jnp.sum(sc * wh, axis=0, keepdims=True)
    sc_new = jnp.maximum(jnp.sum(iqf * new_ref[0], axis=-1, keepdims=True) * IDX_DH ** -0.5, 0.0)
    score_new = jnp.sum(sc_new * wh, axis=0, keepdims=True)
    lane = lax.broadcasted_iota(I32, (1, LANES), 1)
    row_scr[:, past:past + LANES] = jnp.where(lane == 0, score_new, -3e38)
    o_ref[0] = _topk_additive(row_scr[...], key_scr, topk)


def _idx_sample(iq, iw, pool, layer, page_table, new_rows, topk):
    ns = iq.shape[0]
    n_pages = page_table.shape[1]
    past = n_pages * PAGE
    w = pool.shape[3]
    return pl.pallas_call(
        functools.partial(_idx_sample_kernel, layer=layer, topk=topk),
        grid_spec=pltpu.PrefetchScalarGridSpec(
            num_scalar_prefetch=1, grid=(ns,),
            in_specs=[pl.BlockSpec((1, SROWS, IDX_DH), lambda s, pt: (s, 0, 0)),
                      pl.BlockSpec((1, SROWS, LANES), lambda s, pt: (s, 0, 0)),
                      pl.BlockSpec(memory_space=pl.ANY),
                      pl.BlockSpec((1, 1, w), lambda s, pt: (s, 0, 0))],
            out_specs=pl.BlockSpec((1, 1, past + LANES), lambda s, pt: (s, 0, 0)),
            scratch_shapes=[pltpu.VMEM((2, n_pages, PAGE, w), F32), pltpu.SemaphoreType.DMA((2,)),
                            pltpu.VMEM((1, past + LANES), F32), pltpu.VMEM((1, past + LANES), I32)]),
        out_shape=jax.ShapeDtypeStruct((ns, 1, past + LANES), F32),
        compiler_params=_cp(("arbitrary",)), name="dsa_index_sample")(page_table, iq, iw, pool, new_rows)


def _diff_post_kernel(o_ref, lam_ref, g_ref, out_ref, *, lam_init):
    lv = lam_ref[...]
    lam = (jnp.exp(jnp.sum(lv[0:1] * lv[1:2], axis=-1, keepdims=True))
           - jnp.exp(jnp.sum(lv[2:3] * lv[3:4], axis=-1, keepdims=True)) + lam_init)
    o = o_ref[...]
    d = o[:, 0:N_HEADS, :] - lam * o[:, N_HEADS:2 * N_HEADS, :]
    d = d * lax.rsqrt(jnp.mean(d * d, axis=-1, keepdims=True) + LN_EPS)
    out_ref[...] = d * g_ref[...] * (1.0 - lam_init)


def _diff_post(o, lam, g, lam_init):
    ns = o.shape[0]
    vm = pl.BlockSpec(memory_space=pltpu.VMEM)
    return pl.pallas_call(
        functools.partial(_diff_post_kernel, lam_init=lam_init), in_specs=[vm, vm, vm], out_specs=vm,
        out_shape=jax.ShapeDtypeStruct((ns, N_HEADS, LANES), F32),
        compiler_params=_cp(None), name="diff_post_sample")(o, lam, g.reshape(1, LANES))


def _rope_group(x, cos, sin):
    lane = lax.broadcasted_iota(I32, x.shape, 1)
    half = MLA_ROPE // 2
    other = jnp.where(lane < half, pltpu.roll(x, LANES - half, 1), pltpu.roll(x, half, 1))
    return x * cos + other * sin


def _mla_prep_kernel(h_ref, qg_ref, kg_ref, cos_ref, sin_ref, qn_ref, kv_ref):
    qa = h_ref[:, 0:MLA_Q_RANK]
    qn = qa * lax.rsqrt(jnp.mean(qa * qa, axis=-1, keepdims=True) + LN_EPS) * qg_ref[...]
    qn_ref[...] = qn.astype(BF16)
    kva = h_ref[:, MLA_Q_RANK:MLA_Q_RANK + MLA_KV_RANK]
    kv_ref[:, 0:MLA_KV_RANK] = kva * lax.rsqrt(jnp.mean(kva * kva, axis=-1, keepdims=True) + LN_EPS) * kg_ref[...]
    kr = h_ref[:, MLA_Q_RANK + MLA_KV_RANK + LANES:MLA_Q_RANK + MLA_KV_RANK + 2 * LANES]
    kv_ref[:, MLA_KV_RANK:MLA_KV_RANK + LANES] = _rope_group(kr, cos_ref[...], sin_ref[...])


def _mla_prep(h, qg, kg, cos, sin):
    m = h.shape[0]
    tm = _tile(m, 640, 16)
    wide = MLA_Q_RANK + MLA_KV_RANK + 2 * LANES
    return pl.pallas_call(
        _mla_prep_kernel, grid=(m // tm,),
        in_specs=[pl.BlockSpec((tm, wide), lambda i: (i, 1)),
                  pl.BlockSpec((1, MLA_Q_RANK), lambda i: (0, 0)), pl.BlockSpec((1, MLA_KV_RANK), lambda i: (0, 0)),
                  pl.BlockSpec((tm, LANES), lambda i: (i, 0)), pl.BlockSpec((tm, LANES), lambda i: (i, 0))],
        out_specs=[pl.BlockSpec((tm, MLA_Q_RANK), lambda i: (i, 0)),
                   pl.BlockSpec((tm, MLA_KV_RANK + LANES), lambda i: (i, 0))],
        out_shape=[jax.ShapeDtypeStruct((m, MLA_Q_RANK), BF16),
                   jax.ShapeDtypeStruct((m, MLA_KV_RANK + LANES), F32)],
        compiler_params=_cp(("parallel",)), name="mla_prep")(
            h, qg.reshape(1, -1), kg.reshape(1, -1), cos, sin)


def _mla_q_kernel(qf_ref, wuk_ref, cos_ref, sin_ref, o_ref):
    width = MLA_KV_RANK + LANES
    for h in range(N_HEADS):
        nope = qf_ref[:, h * LANES:(h + 1) * LANES].astype(BF16)
        o_ref[:, h * width:h * width + MLA_KV_RANK] = jnp.dot(
            nope, wuk_ref[h], preferred_element_type=F32).astype(o_ref.dtype)
        rope = qf_ref[:, (N_HEADS + h) * LANES:(N_HEADS + h + 1) * LANES]
        o_ref[:, h * width + MLA_KV_RANK:(h + 1) * width] = _rope_group(
            rope, cos_ref[...], sin_ref[...]).astype(o_ref.dtype)


def _mla_q(qf, wuk_t, cos, sin, out_dtype):
    m = qf.shape[0]
    tm = _tile(m, 640, 16)
    width = N_HEADS * (MLA_KV_RANK + LANES)
    return pl.pallas_call(
        _mla_q_kernel, grid=(m // tm,),
        in_specs=[pl.BlockSpec((tm, qf.shape[1]), lambda i: (i, 0)),
                  pl.BlockSpec(wuk_t.shape, lambda i: (0, 0, 0)),
                  pl.BlockSpec((tm, LANES), lambda i: (i, 0)), pl.BlockSpec((tm, LANES), lambda i: (i, 0))],
        out_specs=pl.BlockSpec((tm, width), lambda i: (i, 0)),
        out_shape=jax.ShapeDtypeStruct((m, width), out_dtype),
        compiler_params=_cp(("parallel",)), name="mla_q")(qf, wuk_t, cos, sin)


def _mla_uv_kernel(o_ref, w_ref, d_ref):
    for h in range(N_HEADS):
        d_ref[:, h * MLA_VDH:(h + 1) * MLA_VDH] = jnp.dot(
            o_ref[:, h * MLA_KV_RANK:(h + 1) * MLA_KV_RANK], w_ref[h], preferred_element_type=F32).astype(BF16)


def _mla_uv(o_lat, wuv):
    m = o_lat.shape[0]
    tm = _tile(m, 640, 16)
    return pl.pallas_call(
        _mla_uv_kernel, grid=(m // tm,),
        in_specs=[pl.BlockSpec((tm, o_lat.shape[1]), lambda i: (i, 0)), pl.BlockSpec(wuv.shape, lambda i: (0, 0, 0))],
        out_specs=pl.BlockSpec((tm, N_HEADS * MLA_VDH), lambda i: (i, 0)),
        out_shape=jax.ShapeDtypeStruct((m, N_HEADS * MLA_VDH), BF16),
        compiler_params=_cp(("parallel",)), name="mla_uv")(o_lat, wuv)


def _prep_w_in_even(w):
    d = w.shape[0]
    pad = jnp.zeros((d, 3200 - 3144), w.dtype)
    cols = [w[:, 0:1024], w[:, 1280:2304], w[:, 2560:3072], w[:, 1024:1280], w[:, 2304:2560], w[:, 3072:3144], pad]
    return jnp.concatenate(cols, axis=1).astype(BF16)


def _prep_w_in_odd(w):
    d = w.shape[0]
    mq = jnp.pad(w[:, 0:512].reshape(d, N_HEADS, MOBA_DH), ((0, 0), (0, 0), (0, LANES - MOBA_DH))).reshape(d, -1)
    kr = jnp.pad(w[:, 1408:1472], ((0, 0), (0, LANES - MLA_ROPE)))
    return jnp.concatenate([mq, w[:, 640:1152], w[:, 1152:1408], w[:, 512:640], kr], axis=1).astype(BF16)


def _prep_w_q_up(w):
    r = w.shape[0]
    w3 = w.reshape(r, N_HEADS, MLA_NOPE + MLA_ROPE)
    nope = w3[:, :, :MLA_NOPE].reshape(r, -1)
    rope = jnp.pad(w3[:, :, MLA_NOPE:], ((0, 0), (0, 0), (0, LANES - MLA_ROPE))).reshape(r, -1)
    return jnp.concatenate([nope, rope], axis=1).astype(BF16)


def _rope_tables(pos):
    half = MLA_ROPE // 2
    freq = jnp.power(ROPE_THETA, -jnp.arange(half, dtype=F32) / half)
    ang = pos.astype(F32)[:, None] * freq
    cos, sin = jnp.cos(ang), jnp.sin(ang)
    zero = jnp.zeros((pos.shape[0], LANES - MLA_ROPE), F32)
    return jnp.concatenate([cos, cos, zero], axis=1), jnp.concatenate([-sin, sin, zero], axis=1)


def _sample_rows(x, n_rows=SROWS):
    return jnp.pad(x, ((0, 0), (0, n_rows - x.shape[1]), (0, 0)))


def kernel(x_prompt, x_sample, cache_diff_kv, cache_dsa_kv, cache_dsa_idx, cache_moba_kv, cache_mla, page_table, p_prompt, p_sample, rel_bias, w_in_even, diff_lambda, diff_subln_g, w_out_even, ffn_w_in, ffn_w_out, w_in_odd, mla_q_norm_g, mla_kv_norm_g, mla_w_q_up, mla_w_uk, mla_w_uv, w_out_odd, moe_router, moe_w_in, moe_w_out, ln1_g, ln1_b, ln2_g, ln2_b, ple_w_gate, ple_w_proj):
    b, l, d = x_prompt.shape
    ns, dec_seq, _ = x_sample.shape
    depth = ln1_g.shape[0]
    n_pages = page_table.shape[1]
    past = n_pages * PAGE
    n_prompt = b * l
    assert dec_seq == 1 and past % MOBA_BLOCK == 0 and n_pages % CHUNK_PAGES == 0
    assert l % TK == 0 and TK == MOBA_BLOCK and TK == 2 * TQ
    alpha = (2 * depth) ** 0.25

    x = jnp.concatenate([x_prompt.reshape(n_prompt, d), x_sample.reshape(ns, d)], axis=0)
    xb = x.astype(BF16)
    ple = jnp.concatenate([p_prompt.reshape(depth, n_prompt, -1), p_sample.reshape(depth, ns, -1)], axis=1).astype(BF16)

    tbl8 = _bias_tables_prompt(rel_bias)
    tbl0 = _bias_tables_prompt(jnp.zeros((REL_BUCKETS, 1), F32))
    tbs = _bias_table_sample(rel_bias, past)
    pos = jnp.concatenate([jnp.tile(jnp.arange(l), b), jnp.full((ns,), past)])
    cos, sin = _rope_tables(pos)
    lane = jnp.arange(LANES)

    rows_diff, rows_dsa, rows_idx, rows_moba, rows_mla = [], [], [], [], []
    for i in range(depth):
        j = i // 2
        if i % 2 == 0:
            lam_init = 0.8 - 0.6 * math.exp(-0.3 * i)
            h = _matmul(xb, _prep_w_in_even(w_in_even[j]), F32, "in_proj_even", tn_pref=640)
            r_diff, r_dsa, r_idx = h[:, 2560:2816], h[:, 2816:3072], h[:, 3072:3072 + IDX_DH]
            rows_diff.append(r_diff); rows_dsa.append(r_dsa); rows_idx.append(r_idx)
            hp = h[:n_prompt].reshape(b, l, -1)
            hs = h[n_prompt:]
            a_p = _flash("A", hp, hp, tbl8, q_blk=0, q_width=1024, k_blk=20, k_width=LANES, v_blk=21, dv=LANES,
                         out_width=1024, groups=2 * N_HEADS, scale=DIFF_DH ** -0.5, lam=diff_lambda[j],
                         subln_g=diff_subln_g[j], lam_init=lam_init, name="diff_attn_prompt")
            topk = min(DSA_TOPK, l // 4)
            sel = _idx_prompt(hp, topk)
            s_p = _flash("B", hp, hp, tbl8, q_blk=1, q_width=1024, k_blk=22, k_width=LANES, v_blk=23, dv=LANES,
                         out_width=1024, groups=N_HEADS, scale=DSA_DH ** -0.5, mask=sel, name="dsa_attn_prompt")
            dq = hs[:, 0:1024].reshape(ns, N_HEADS, LANES)
            q_a = jnp.concatenate([jnp.where(lane < DIFF_DH, dq, 0.0), jnp.where(lane >= DIFF_DH, dq, 0.0)], axis=1)
            o_a = _sample_attn("A", q_a, cache_diff_kv, j, page_table, hs[:, None, 2560:2816], tbs,
                               pieces=((0, LANES),), v_lo=LANES, v_hi=2 * LANES, scale=DIFF_DH ** -0.5,
                               name="diff_attn_sample")
            a_s = _diff_post(o_a, diff_lambda[j], diff_subln_g[j], lam_init).reshape(ns, -1)
            iq = _sample_rows(hs[:, 2048:2560].reshape(ns, IDX_HEADS, IDX_DH))
            iw = _sample_rows(jnp.broadcast_to(hs[:, 3072 + IDX_DH:3072 + IDX_DH + IDX_HEADS, None],
                                               (ns, IDX_HEADS, LANES)))
            topk_s = min(DSA_TOPK, (past + 1) // 4)
            sel_s = _idx_sample(iq, iw, cache_dsa_idx, j, page_table, hs[:, None, 3072:3072 + IDX_DH], topk_s)
            q_s = _sample_rows(hs[:, 1024:2048].reshape(ns, N_HEADS, LANES))
            o_s = _sample_attn("B", q_s, cache_dsa_kv, j, page_table, hs[:, None, 2816:3072], tbs,
                               pieces=((0, LANES),), v_lo=LANES, v_hi=2 * LANES, scale=DSA_DH ** -0.5,
                               extra=sel_s, name="dsa_attn_sample")
            s_s = o_s[:, :N_HEADS].reshape(ns, -1)
            mix_p = jnp.concatenate([a_p.reshape(n_prompt, -1), s_p.reshape(n_prompt, -1)], axis=1)
            mix_s = jnp.concatenate([a_s, s_s], axis=1).astype(BF16)
            mix = jnp.concatenate([mix_p, mix_s], axis=0)
            hout = _matmul(mix, w_out_even[j].astype(BF16), F32, "out_proj_even")
        else:
            h = _matmul(xb, _prep_w_in_odd(w_in_odd[j]), F32, "in_proj_odd")
            r_moba = h[:, 1792:1920]
            rows_moba.append(r_moba)
            qn, kvrow = _mla_prep(h, mla_q_norm_g[j], mla_kv_norm_g[j], cos, sin)
            rows_mla.append(kvrow[:, :MLA_KV_RANK + MLA_ROPE])
            qf = _matmul(qn, _prep_w_q_up(mla_w_q_up[j]), F32, "mla_q_up")
            wuk_t = jnp.transpose(mla_w_uk[j], (1, 2, 0)).astype(BF16)
            wuv = jnp.transpose(mla_w_uv[j], (1, 0, 2)).astype(BF16)
            qmla = _mla_q(qf, wuk_t, cos, sin, F32)
            hp = h[:n_prompt].reshape(b, l, -1)
            hs = h[n_prompt:]
            m_p = _flash("C", hp, hp, tbl8, q_blk=0, q_width=1024, k_blk=14, k_width=LANES, dv=LANES,
                         out_width=N_HEADS * MOBA_DH, groups=N_HEADS, scale=MOBA_DH ** -0.5,
                         n_sel=min(MOBA_TOPK, l // MOBA_BLOCK), name="moba_attn_prompt")
            width = MLA_KV_RANK + LANES
            o_lat_p = _flash("D", qmla[:n_prompt].astype(BF16).reshape(b, l, -1), kvrow[:n_prompt].reshape(b, l, -1),
                             tbl0, q_blk=0, q_width=N_HEADS * width, k_blk=0, k_width=width, dv=MLA_KV_RANK,
                             out_width=N_HEADS * MLA_KV_RANK, groups=N_HEADS,
                             scale=(MLA_NOPE + MLA_ROPE) ** -0.5, name="mla_attn_prompt")
            q_c = _sample_rows(hs[:, 0:1024].reshape(ns, N_HEADS, LANES))
            o_c = _sample_attn("C", q_c, cache_moba_kv, j, page_table, hs[:, None, 1792:1920], tbs,
                               pieces=((0, LANES),), v_lo=0, v_hi=LANES, scale=MOBA_DH ** -0.5,
                               n_sel=min(MOBA_TOPK, (past + 1) // MOBA_BLOCK), name="moba_attn_sample")
            m_s = o_c[:, :N_HEADS, MOBA_DH:].reshape(ns, -1)
            q_d = _sample_rows(qmla[n_prompt:].reshape(ns, N_HEADS, width)[:, :, :MLA_KV_RANK + MLA_ROPE])
            o_d = _sample_attn("D", q_d, cache_mla, j, page_table, kvrow[n_prompt:, None, :MLA_KV_RANK + MLA_ROPE],
                               None, pieces=((0, MLA_KV_RANK), (MLA_KV_RANK, MLA_KV_RANK + MLA_ROPE)),
                               v_lo=0, v_hi=MLA_KV_RANK, scale=(MLA_NOPE + MLA_ROPE) ** -0.5, name="mla_attn_sample")
            o_lat = jnp.concatenate([o_lat_p.reshape(n_prompt, -1),
                                     o_d[:, :N_HEADS].reshape(ns, -1).astype(BF16)], axis=0)
            dmla = _mla_uv(o_lat, wuv)
            m_all = jnp.concatenate([m_p.reshape(n_prompt, -1), m_s.astype(BF16)], axis=0)
            mix = jnp.concatenate([m_all, dmla], axis=1)
            hout = _matmul(mix, w_out_odd[j].astype(BF16), F32, "out_proj_odd")

        x, xb = _deepnorm_ln(x, hout, ln1_g[i], ln1_b[i], alpha, "deepnorm_ln1")
        if i % 2 == 0:
            mid = _swiglu(xb, ffn_w_in[j][None].astype(BF16), "ffn_in")
            f = _matmul_k(mid, ffn_w_out[j].astype(BF16), "ffn_out")
        else:
            router = jnp.pad(moe_router[j], ((0, 0), (0, LANES - N_EXPERTS))).astype(BF16)
            combine = _router(xb, router, "moe_router")
            mid = _swiglu(xb, moe_w_in[j].astype(BF16), "moe_in")
            w_out = moe_w_out[j].astype(BF16)
            f = _matmul_k(mid, w_out.reshape(-1, d), "moe_out", combine=combine, n_groups=w_out.shape[0])
        x, xb = _deepnorm_ln(x, f, ln2_g[i], ln2_b[i], alpha, "deepnorm_ln2")
        x, xb = _ple(xb, x, ple_w_gate[i].astype(BF16), ple[i], ple_w_proj[i].astype(BF16), "ple_gate")

    def split(rows, width):
        r = jnp.stack(rows, axis=0)
        rp = jnp.transpose(r[:, :n_prompt].reshape(-1, b, l, width), (1, 0, 2, 3))
        rs = jnp.transpose(r[:, n_prompt:].reshape(-1, ns, 1, width), (1, 0, 2, 3))
        return rp, rs

    dp, ds = split(rows_diff, 256)
    sp, ss = split(rows_dsa, 256)
    ip, is_ = split(rows_idx, IDX_DH)
    mp, ms = split(rows_moba, 2 * MOBA_DH)
    lp, ls = split(rows_mla, MLA_KV_RANK + MLA_ROPE)
    return (x[:n_prompt].reshape(b, l, d), x[n_prompt:].reshape(ns, 1, d), dp, ds, sp, ss, ip, is_, mp, ms, lp, ls)
```

```python
import functools
import math

import jax
import jax.numpy as jnp
from jax import lax
from jax.experimental import pallas as pl
from jax.experimental.pallas import tpu as pltpu

F32, BF16, I32 = jnp.float32, jnp.bfloat16, jnp.int32

N_HEADS = 8
DIFF_DH = 64
DSA_DH = 128
IDX_HEADS = 8
IDX_DH = 64
DSA_TOPK = 256
MOBA_DH = 64
MOBA_BLOCK = 256
MOBA_TOPK = 3
MLA_Q_RANK = 512
MLA_KV_RANK = 256
MLA_NOPE = 128
MLA_ROPE = 64
MLA_VDH = 128
ROPE_THETA = 10000.0
REL_BUCKETS = 32
REL_MAX_DIST = 128
N_EXPERTS = 8
PAGE = 128
NEG = -1e30
LN_EPS = 1e-5
INT_MIN = -(2 ** 31)

LANES = 128
TQ = 128
TK = 256
SROWS = 16
CHUNK_PAGES = 16
VMEM_LIMIT = 52 * 1024 * 1024


def _cp(sem, vmem=VMEM_LIMIT):
    return pltpu.CompilerParams(dimension_semantics=sem, vmem_limit_bytes=vmem)


def _tile(n, pref, mult):
    best = None
    for t in range(mult, min(n, pref) + 1, mult):
        if n % t == 0:
            best = t
    return best if best is not None else n


def _mm_kernel(x_ref, w_ref, o_ref):
    o_ref[...] = jnp.dot(x_ref[...], w_ref[...], preferred_element_type=F32).astype(o_ref.dtype)


def _matmul(x, w, out_dtype, name, tn_pref=512):
    m, k = x.shape
    n = w.shape[1]
    tm = _tile(m, 640, 16)
    tn = _tile(n, tn_pref, LANES)
    return pl.pallas_call(
        _mm_kernel, grid=(m // tm, n // tn),
        in_specs=[pl.BlockSpec((tm, k), lambda i, j: (i, 0)), pl.BlockSpec((k, tn), lambda i, j: (0, j))],
        out_specs=pl.BlockSpec((tm, tn), lambda i, j: (i, j)),
        out_shape=jax.ShapeDtypeStruct((m, n), out_dtype),
        compiler_params=_cp(("parallel", "arbitrary")), name=name)(x, w)


def _swiglu_kernel(x_ref, wg_ref, wu_ref, o_ref):
    x = x_ref[...]
    g = jnp.dot(x, wg_ref[...], preferred_element_type=F32)
    u = jnp.dot(x, wu_ref[...], preferred_element_type=F32)
    o_ref[...] = (g * jax.nn.sigmoid(g) * u).astype(o_ref.dtype)


def _swiglu(x, w_in, name):
    m, k = x.shape
    e, _, f2 = w_in.shape
    f = f2 // 2
    tm = _tile(m, 640, 16)
    tn = _tile(f, 512, LANES)
    nper = f // tn
    return pl.pallas_call(
        _swiglu_kernel, grid=(m // tm, e * nper),
        in_specs=[pl.BlockSpec((tm, k), lambda i, j: (i, 0)),
                  pl.BlockSpec((None, k, tn), lambda i, j: (j // nper, 0, j % nper)),
                  pl.BlockSpec((None, k, tn), lambda i, j: (j // nper, 0, j % nper + nper))],
        out_specs=pl.BlockSpec((tm, tn), lambda i, j: (i, j)),
        out_shape=jax.ShapeDtypeStruct((m, e * f), BF16),
        compiler_params=_cp(("parallel", "arbitrary")), name=name)(x, w_in, w_in)


def _mmk_kernel(h_ref, w_ref, o_ref, acc_ref):
    k = pl.program_id(2)

    @pl.when(k == 0)
    def _():
        acc_ref[...] = jnp.zeros_like(acc_ref)

    acc_ref[...] += jnp.dot(h_ref[...], w_ref[...], preferred_element_type=F32)

    @pl.when(k == pl.num_programs(2) - 1)
    def _():
        o_ref[...] = acc_ref[...]


def _matmul_k(h, w, name):
    m, k = h.shape
    n = w.shape[1]
    tm = _tile(m, 640, 16)
    tn = _tile(n, 1024, LANES)
    tk = _tile(k, 512, LANES)
    return pl.pallas_call(
        _mmk_kernel, grid=(m // tm, n // tn, k // tk),
        in_specs=[pl.BlockSpec((tm, tk), lambda i, j, kk: (i, kk)), pl.BlockSpec((tk, tn), lambda i, j, kk: (kk, j))],
        out_specs=pl.BlockSpec((tm, tn), lambda i, j, kk: (i, j)),
        out_shape=jax.ShapeDtypeStruct((m, n), F32), scratch_shapes=[pltpu.VMEM((tm, tn), F32)],
        compiler_params=_cp(("parallel", "parallel", "arbitrary")), name=name)(h, w)


def _ln_kernel(x_ref, h_ref, g_ref, b_ref, o_ref, ob_ref, *, alpha):
    z = alpha * x_ref[...] + h_ref[...]
    mu = jnp.mean(z, axis=-1, keepdims=True)
    zc = z - mu
    var = jnp.mean(zc * zc, axis=-1, keepdims=True)
    y = zc * lax.rsqrt(var + LN_EPS) * g_ref[...] + b_ref[...]
    o_ref[...] = y
    ob_ref[...] = y.astype(BF16)


def _deepnorm_ln(x, h, g, b, alpha, name):
    m, d = x.shape
    tm = _tile(m, 320, 16)
    row = pl.BlockSpec((tm, d), lambda i: (i, 0))
    vec = pl.BlockSpec((1, d), lambda i: (0, 0))
    return pl.pallas_call(
        functools.partial(_ln_kernel, alpha=alpha), grid=(m // tm,),
        in_specs=[row, row, vec, vec], out_specs=[row, row],
        out_shape=[jax.ShapeDtypeStruct((m, d), F32), jax.ShapeDtypeStruct((m, d), BF16)],
        compiler_params=_cp(("parallel",)), name=name)(x, h, g.reshape(1, d), b.reshape(1, d))


def _ple_kernel(xb_ref, x_ref, wg_ref, pb_ref, wp_ref, o_ref, ob_ref):
    gate = jax.nn.sigmoid(jnp.dot(xb_ref[...], wg_ref[...], preferred_element_type=F32))
    proj = jnp.dot(pb_ref[...], wp_ref[...], preferred_element_type=F32)
    y = x_ref[...] + gate * proj
    o_ref[...] = y
    ob_ref[...] = y.astype(BF16)


def _ple(xb, x, wg, pb, wp, name):
    m, d = x.shape
    pd = pb.shape[1]
    tm = _tile(m, 640, 16)
    tn = _tile(d, 512, LANES)
    return pl.pallas_call(
        _ple_kernel, grid=(m // tm, d // tn),
        in_specs=[pl.BlockSpec((tm, d), lambda i, j: (i, 0)), pl.BlockSpec((tm, tn), lambda i, j: (i, j)),
                  pl.BlockSpec((d, tn), lambda i, j: (0, j)), pl.BlockSpec((tm, pd), lambda i, j: (i, 0)),
                  pl.BlockSpec((pd, tn), lambda i, j: (0, j))],
        out_specs=[pl.BlockSpec((tm, tn), lambda i, j: (i, j)), pl.BlockSpec((tm, tn), lambda i, j: (i, j))],
        out_shape=[jax.ShapeDtypeStruct((m, d), F32), jax.ShapeDtypeStruct((m, d), BF16)],
        compiler_params=_cp(("parallel", "arbitrary")), name=name)(xb, x, wg, pb, wp)


def _router_kernel(xb_ref, w_ref, o_ref):
    logits = jnp.dot(xb_ref[...], w_ref[...], preferred_element_type=F32)
    lane = lax.broadcasted_iota(I32, logits.shape, 1)
    logits = jnp.where(lane < N_EXPERTS, logits, -jnp.inf)
    m1 = jnp.max(logits, axis=-1, keepdims=True)
    i1 = jnp.min(jnp.where(logits == m1, lane, LANES), axis=-1, keepdims=True)
    rest = jnp.where(lane == i1, -jnp.inf, logits)
    m2 = jnp.max(rest, axis=-1, keepdims=True)
    i2 = jnp.min(jnp.where(rest == m2, lane, LANES), axis=-1, keepdims=True)
    e2 = jnp.exp(m2 - m1)
    den = 1.0 + e2
    o_ref[...] = jnp.where(lane == 0, i1.astype(F32),
                           jnp.where(lane == 1, i2.astype(F32),
                                     jnp.where(lane == 2, 1.0 / den, jnp.where(lane == 3, e2 / den, 0.0))))


def _router(xb, w, name):
    m, d = xb.shape
    tm = _tile(m, 640, 16)
    return pl.pallas_call(
        _router_kernel, grid=(m // tm,),
        in_specs=[pl.BlockSpec((tm, d), lambda i: (i, 0)), pl.BlockSpec((d, LANES), lambda i: (0, 0))],
        out_specs=pl.BlockSpec((tm, LANES), lambda i: (i, 0)),
        out_shape=jax.ShapeDtypeStruct((m, LANES), F32),
        compiler_params=_cp(("parallel",)), name=name)(xb, w)


MOE_TILE = 512
GATHER_ROWS = 256


def _row_copy(src_hbm, src_row, dst, dst_row, sem):
    return pltpu.make_async_copy(src_hbm.at[pl.ds(src_row, 1)], dst.at[pl.ds(dst_row, 1)], sem)


def _gather_rows_kernel(tok_ref, x_hbm, o_ref, buf, sem):
    base = pl.program_id(0) * buf.shape[0]

    def issue(r, c):
        _row_copy(x_hbm, tok_ref[base + r], buf, r, sem).start()
        return c

    def drain(r, c):
        _row_copy(x_hbm, 0, buf, r, sem).wait()
        return c

    lax.fori_loop(0, buf.shape[0], issue, 0)
    lax.fori_loop(0, buf.shape[0], drain, 0)
    o_ref[...] = buf[...].astype(o_ref.dtype)


def _gather_rows(x, token_of_slot):
    p = token_of_slot.shape[0]
    d = x.shape[1]
    return pl.pallas_call(
        _gather_rows_kernel,
        grid_spec=pltpu.PrefetchScalarGridSpec(
            num_scalar_prefetch=1, grid=(p // GATHER_ROWS,),
            in_specs=[pl.BlockSpec(memory_space=pl.ANY)],
            out_specs=pl.BlockSpec((GATHER_ROWS, d), lambda i, tok: (i, 0)),
            scratch_shapes=[pltpu.VMEM((GATHER_ROWS, d), x.dtype), pltpu.SemaphoreType.DMA(())]),
        out_shape=jax.ShapeDtypeStruct((p, d), BF16),
        compiler_params=_cp(("arbitrary",)), name="moe_gather")(token_of_slot, x)


def _gswiglu_kernel(te_ref, nu_ref, x_ref, wg_ref, wu_ref, o_ref):
    @pl.when(pl.program_id(1) < nu_ref[0])
    def _():
        x = x_ref[...]
        g = jnp.dot(x, wg_ref[...].astype(BF16), preferred_element_type=F32)
        u = jnp.dot(x, wu_ref[...].astype(BF16), preferred_element_type=F32)
        o_ref[...] = (g * jax.nn.sigmoid(g) * u).astype(o_ref.dtype)

    @pl.when(pl.program_id(1) >= nu_ref[0])
    def _():
        o_ref[...] = jnp.zeros_like(o_ref)


def _grouped_swiglu(xs, w_in, tile_expert, n_used):
    p, k = xs.shape
    f = w_in.shape[2] // 2
    tn = _tile(f, 512, LANES)
    nper = f // tn
    return pl.pallas_call(
        _gswiglu_kernel,
        grid_spec=pltpu.PrefetchScalarGridSpec(
            num_scalar_prefetch=2, grid=(nper, p // MOE_TILE),
            in_specs=[pl.BlockSpec((MOE_TILE, k), lambda j, i, te, nu: (i, 0)),
                      pl.BlockSpec((None, k, tn), lambda j, i, te, nu: (te[i], 0, j)),
                      pl.BlockSpec((None, k, tn), lambda j, i, te, nu: (te[i], 0, j + nper))],
            out_specs=pl.BlockSpec((MOE_TILE, tn), lambda j, i, te, nu: (i, j))),
        out_shape=jax.ShapeDtypeStruct((p, f), BF16),
        compiler_params=_cp(("arbitrary", "arbitrary")), name="moe_in")(tile_expert, n_used, xs, w_in, w_in)


def _gmmk_kernel(te_ref, nu_ref, h_ref, w_ref, o_ref, acc_ref):
    k = pl.program_id(2)

    @pl.when(k == 0)
    def _():
        acc_ref[...] = jnp.zeros_like(acc_ref)

    @pl.when(pl.program_id(0) < nu_ref[0])
    def _():
        acc_ref[...] += jnp.dot(h_ref[...], w_ref[...], preferred_element_type=F32)

    @pl.when(k == pl.num_programs(2) - 1)
    def _():
        o_ref[...] = acc_ref[...]


def _grouped_matmul_k(h, w, tile_expert, n_used):
    p, f = h.shape
    n = w.shape[2]
    tn = _tile(n, 1024, LANES)
    tk = _tile(f, 512, LANES)
    return pl.pallas_call(
        _gmmk_kernel,
        grid_spec=pltpu.PrefetchScalarGridSpec(
            num_scalar_prefetch=2, grid=(p // MOE_TILE, n // tn, f // tk),
            in_specs=[pl.BlockSpec((MOE_TILE, tk), lambda i, j, kk, te, nu: (i, kk)),
                      pl.BlockSpec((None, tk, tn), lambda i, j, kk, te, nu: (te[i], kk, j))],
            out_specs=pl.BlockSpec((MOE_TILE, tn), lambda i, j, kk, te, nu: (i, j)),
            scratch_shapes=[pltpu.VMEM((MOE_TILE, tn), F32)]),
        out_shape=jax.ShapeDtypeStruct((p, n), F32),
        compiler_params=_cp(("arbitrary", "arbitrary", "arbitrary")), name="moe_out")(tile_expert, n_used, h, w)


def _moe_combine_ln_kernel(s1_ref, s2_ref, route_ref, x_ref, ys_hbm, g_ref, b_ref, o_ref, ob_ref,
                           buf1, buf2, sem, *, alpha):
    tm = buf1.shape[0]
    base = pl.program_id(0) * tm

    def issue(r, c):
        _row_copy(ys_hbm, s1_ref[base + r], buf1, r, sem).start()
        _row_copy(ys_hbm, s2_ref[base + r], buf2, r, sem).start()
        return c

    def drain(r, c):
        _row_copy(ys_hbm, 0, buf1, r, sem).wait()
        _row_copy(ys_hbm, 0, buf2, r, sem).wait()
        return c

    lax.fori_loop(0, tm, issue, 0)
    lax.fori_loop(0, tm, drain, 0)
    f = route_ref[:, 2:3] * buf1[...] + route_ref[:, 3:4] * buf2[...]
    z = alpha * x_ref[...] + f
    mu = jnp.mean(z, axis=-1, keepdims=True)
    zc = z - mu
    var = jnp.mean(zc * zc, axis=-1, keepdims=True)
    y = zc * lax.rsqrt(var + LN_EPS) * g_ref[...] + b_ref[...]
    o_ref[...] = y
    ob_ref[...] = y.astype(BF16)


def _moe_combine_ln(ys, slot1, slot2, route, x, g, b, alpha):
    m, d = x.shape
    tm = _tile(m, 320, 16)
    row = lambda i, s1, s2: (i, 0)
    vec = pl.BlockSpec((1, d), lambda i, s1, s2: (0, 0))
    return pl.pallas_call(
        functools.partial(_moe_combine_ln_kernel, alpha=alpha),
        grid_spec=pltpu.PrefetchScalarGridSpec(
            num_scalar_prefetch=2, grid=(m // tm,),
            in_specs=[pl.BlockSpec((tm, LANES), row), pl.BlockSpec((tm, d), row),
                      pl.BlockSpec(memory_space=pl.ANY), vec, vec],
            out_specs=[pl.BlockSpec((tm, d), row), pl.BlockSpec((tm, d), row)],
            scratch_shapes=[pltpu.VMEM((tm, d), F32), pltpu.VMEM((tm, d), F32), pltpu.SemaphoreType.DMA(())]),
        out_shape=[jax.ShapeDtypeStruct((m, d), F32), jax.ShapeDtypeStruct((m, d), BF16)],
        compiler_params=_cp(("arbitrary",)), name="moe_combine_ln")(
            slot1, slot2, route, x, ys, g.reshape(1, d), b.reshape(1, d))


def _dispatch_plan(route, n_tiles):
    m = route.shape[0]
    e = jnp.concatenate([route[:, 0], route[:, 1]]).astype(I32)
    onehot = (e[:, None] == jnp.arange(N_EXPERTS)[None, :]).astype(I32)
    pos = jnp.sum((jnp.cumsum(onehot, axis=0) - 1) * onehot, axis=1)
    counts = jnp.sum(onehot, axis=0)
    padded = (counts + MOE_TILE - 1) // MOE_TILE * MOE_TILE
    ends = jnp.cumsum(padded)
    slot = (ends - padded)[e] + pos
    token = jnp.tile(jnp.arange(m, dtype=I32), 2)
    token_of_slot = jnp.zeros((n_tiles * MOE_TILE,), I32).at[slot].set(token)
    n_used = (ends[-1] // MOE_TILE).astype(I32)
    tile_start = jnp.arange(n_tiles, dtype=I32) * MOE_TILE
    tile_expert = jnp.sum((tile_start[:, None] >= ends[None, :]).astype(I32), axis=1)
    last = jnp.sum((ends[-1] - 1 >= ends).astype(I32))
    tile_expert = jnp.minimum(tile_expert, last).astype(I32)
    return slot[:m].astype(I32), slot[m:].astype(I32), token_of_slot, tile_expert, n_used.reshape(1)


def _t5_bucket(d):
    n = jnp.maximum(d, 0)
    max_exact = REL_BUCKETS // 2
    nf = jnp.maximum(n, 1).astype(F32)
    large = max_exact + (jnp.log(nf / max_exact) / math.log(REL_MAX_DIST / max_exact)
                         * (REL_BUCKETS - max_exact)).astype(I32)
    large = jnp.minimum(large, REL_BUCKETS - 1)
    return jnp.where(n < max_exact, n, large)


def _bias_rows(bucket, rb_ref, h):
    acc = jnp.zeros(bucket.shape, F32)
    for b in range(REL_BUCKETS):
        acc = jnp.where(bucket == b, rb_ref[b, h], acc)
    return acc


def _tbl_prompt_kernel(rb_ref, o_ref, *, hb):
    t = pl.program_id(0)
    off = (t // 2) * TK + (t % 2) * TQ
    r = lax.broadcasted_iota(I32, (TQ, TK), 0)
    c = lax.broadcasted_iota(I32, (TQ, TK), 1)
    d = off + r - c
    bucket = _t5_bucket(d)
    for h in range(hb):
        o_ref[0, h * TQ:(h + 1) * TQ, :] = jnp.where(d >= 0, _bias_rows(bucket, rb_ref, h), NEG)


def _bias_tables_prompt(rb):
    hb = rb.shape[1]
    return pl.pallas_call(
        functools.partial(_tbl_prompt_kernel, hb=hb), grid=(6,),
        in_specs=[pl.BlockSpec(memory_space=pltpu.SMEM)],
        out_specs=pl.BlockSpec((1, hb * TQ, TK), lambda t: (t, 0, 0)),
        out_shape=jax.ShapeDtypeStruct((6, hb * TQ, TK), F32),
        compiler_params=_cp(("arbitrary",)), name="bias_tables_prompt")(rb)


def _tbl_sample_kernel(rb_ref, o_ref, *, past):
    k = lax.broadcasted_iota(I32, (1, o_ref.shape[1]), 1)
    bucket = _t5_bucket(past - k)
    for h in range(N_HEADS):
        row = _bias_rows(bucket, rb_ref, h)
        o_ref[h:h + 1, :] = row
        o_ref[N_HEADS + h:N_HEADS + h + 1, :] = row


def _bias_table_sample(rb, past):
    w = past + LANES
    return pl.pallas_call(
        functools.partial(_tbl_sample_kernel, past=past),
        in_specs=[pl.BlockSpec(memory_space=pltpu.SMEM)],
        out_specs=pl.BlockSpec(memory_space=pltpu.VMEM),
        out_shape=jax.ShapeDtypeStruct((SROWS, w), F32),
        compiler_params=_cp(None), name="bias_table_sample")(rb)


def _topk_additive(score, key_ref, k):
    r, n = score.shape
    score = jnp.where(score == 0.0, 0.0, score)
    bits = pltpu.bitcast(score, I32)
    key_ref[...] = jnp.where(bits < 0, bits ^ 0x7FFFFFFF, bits)
    kf = float(k)

    def count(cond):
        return jnp.sum(jnp.where(cond, 1.0, 0.0), axis=-1, keepdims=True)

    t0 = jnp.where(count(key_ref[...] >= 0) >= kf, 0, INT_MIN).astype(I32)

    def value_bit(it, t):
        cand = t + lax.shift_left(jnp.int32(1), 30 - it)
        return jnp.where(count(key_ref[...] >= cand) >= kf, cand, t)

    t = lax.fori_loop(0, 31, value_bit, t0)
    need = kf - count(key_ref[...] > t)
    idx_bits = max(1, (n - 1).bit_length())

    def index_bit(it, x):
        cand = x + lax.shift_left(jnp.int32(1), idx_bits - 1 - it)
        idx = lax.broadcasted_iota(I32, (r, n), 1)
        below = count(jnp.logical_and(key_ref[...] == t, idx < cand))
        return jnp.where(below < need, cand, x)

    x = lax.fori_loop(0, idx_bits, index_bit, jnp.zeros((r, 1), I32))
    key = key_ref[...]
    idx = lax.broadcasted_iota(I32, (r, n), 1)
    sel = jnp.logical_or(key > t, jnp.logical_and(key == t, idx <= x))
    return jnp.where(sel, 0.0, NEG)


def _idx_prompt_kernel(iq_ref, kx_ref, wx_ref, o_ref, key_ref, *, topk):
    i = pl.program_id(1)
    lk = kx_ref.shape[1]
    lane = lax.broadcasted_iota(I32, (1, LANES), 1)
    kx = jnp.where(lane < IDX_DH, kx_ref[0], 0.0)
    kk = (kx + pltpu.roll(kx, IDX_DH, 1)).astype(BF16)
    score = jnp.zeros((TQ, lk), F32)
    for h in range(IDX_HEADS):
        slab = iq_ref[0, :, (h // 2) * LANES:(h // 2 + 1) * LANES]
        keep = (lane < IDX_DH) if h % 2 == 0 else (lane >= IDX_DH)
        qh = jnp.where(keep, slab, 0.0).astype(BF16)
        sc = lax.dot_general(qh, kk, (((1,), (1,)), ((), ())), preferred_element_type=F32)
        sc = jnp.maximum(sc * IDX_DH ** -0.5, 0.0)
        wh = wx_ref[0, :, IDX_DH + h:IDX_DH + h + 1] * IDX_HEADS ** -0.5
        score = score + sc * wh
    qpos = i * TQ + lax.broadcasted_iota(I32, (TQ, lk), 0)
    kpos = lax.broadcasted_iota(I32, (TQ, lk), 1)
    score = jnp.where(kpos <= qpos, score, NEG)
    o_ref[0] = _topk_additive(score, key_ref, topk).astype(BF16)


def _idx_prompt(h3, topk):
    b, l, _ = h3.shape
    return pl.pallas_call(
        functools.partial(_idx_prompt_kernel, topk=topk), grid=(b, l // TQ),
        in_specs=[pl.BlockSpec((1, TQ, 512), lambda bi, i: (bi, i, 4)),
                  pl.BlockSpec((1, l, LANES), lambda bi, i: (bi, 0, 24)),
                  pl.BlockSpec((1, TQ, LANES), lambda bi, i: (bi, i, 24))],
        out_specs=pl.BlockSpec((1, TQ, l), lambda bi, i: (bi, i, 0)),
        out_shape=jax.ShapeDtypeStruct((b, l, l), BF16),
        scratch_shapes=[pltpu.VMEM((TQ, l), I32)],
        compiler_params=_cp(("parallel", "arbitrary")), name="dsa_index_prompt")(h3, h3, h3)


def _flash_kernel(*refs, variant, scale, n_sel, lam_init):
    if variant == "A":
        q_ref, k_ref, v_ref, tbl_ref, lam_ref, g_ref, o_ref, q_scr, s_scr, p_scr, m_scr, l_scr, acc_scr = refs
    elif variant == "B":
        q_ref, k_ref, v_ref, tbl_ref, mask_ref, o_ref, q_scr, s_scr, p_scr, m_scr, l_scr, acc_scr = refs
    elif variant == "C":
        q_ref, k_ref, tbl_ref, o_ref, q_scr, s_scr, p_scr, m_scr, l_scr, acc_scr, sel_scr, mean_scr = refs
        v_ref = k_ref
    else:
        q_ref, k_ref, tbl_ref, o_ref, q_scr, s_scr, p_scr, m_scr, l_scr, acc_scr = refs
        v_ref = k_ref
    i = pl.program_id(1)
    rows_total = q_scr.shape[0]
    groups = rows_total // TQ
    hb = tbl_ref.shape[1] // TQ
    dv = acc_scr.shape[1]
    own = (i * TQ) // TK
    parity = ((i * TQ) % TK) // TQ
    lane = lax.broadcasted_iota(I32, (TQ, LANES), 1)

    if variant == "A":
        for c in range(2):
            for h in range(N_HEADS):
                slab = q_ref[0, :, h * LANES:(h + 1) * LANES]
                keep = (lane < DIFF_DH) if c == 0 else (lane >= DIFF_DH)
                g = c * N_HEADS + h
                q_scr[g * TQ:(g + 1) * TQ, :] = jnp.where(keep, slab, 0.0).astype(BF16)
    elif variant in ("B", "C"):
        for h in range(N_HEADS):
            q_scr[h * TQ:(h + 1) * TQ, :] = q_ref[0, :, h * LANES:(h + 1) * LANES].astype(BF16)
    else:
        dq = q_scr.shape[1]
        for h in range(N_HEADS):
            q_scr[h * TQ:(h + 1) * TQ, :] = q_ref[0, :, h * dq:(h + 1) * dq]
    m_scr[...] = jnp.full(m_scr.shape, NEG, F32)
    l_scr[...] = jnp.zeros(l_scr.shape, F32)
    acc_scr[...] = jnp.zeros(acc_scr.shape, F32)

    if variant == "C":
        nblk = k_ref.shape[1] // TK
        mean_scr[...] = jnp.zeros(mean_scr.shape, F32)
        for n in range(nblk):
            mean_scr[n:n + 1, :] = jnp.sum(k_ref[0, n * TK:(n + 1) * TK, :], axis=0, keepdims=True) * (1.0 / TK)
        nb_pad = mean_scr.shape[0]
        nlane = lax.broadcasted_iota(I32, (TQ, nb_pad), 1)
        for h in range(N_HEADS):
            gate = lax.dot_general(q_ref[0, :, h * LANES:(h + 1) * LANES], mean_scr[...],
                                   (((1,), (1,)), ((), ())), preferred_element_type=F32)
            rank = jnp.zeros((TQ, nb_pad), F32)
            for n2 in range(nblk):
                col = gate[:, n2:n2 + 1]
                beats = jnp.where(col > gate, 1.0,
                                  jnp.where(col == gate, jnp.where(n2 < nlane, 1.0, 0.0), 0.0))
                rank = rank + beats * (n2 < own).astype(F32)
            add = jnp.where(nlane < own, jnp.where(rank < n_sel, 0.0, NEG), 0.0)
            for n in range(nblk):
                sel_scr[n, h * TQ:(h + 1) * TQ, :] = add[:, n:n + 1]

    def step(j, carry):
        off = pl.multiple_of(j * TK, TK)
        kb = k_ref[0, pl.ds(off, TK), :].astype(BF16)
        s_scr[...] = lax.dot_general(q_scr[...], kb, (((1,), (1,)), ((), ())), preferred_element_type=F32)
        t = jnp.minimum(own - j, 2) * 2 + parity
        if variant == "B":
            extra = mask_ref[0, :, pl.ds(off, TK)].astype(F32)
        for g in range(groups):
            rows = slice(g * TQ, (g + 1) * TQ)
            hrow = (g % hb) * TQ
            s = s_scr[rows, :] * scale + tbl_ref[t, hrow:hrow + TQ, :]
            if variant == "B":
                s = s + extra
            if variant == "C":
                s = s + sel_scr[j, rows, :]
            m_prev = m_scr[rows, :]
            m_new = jnp.maximum(m_prev, jnp.max(s, axis=-1, keepdims=True))
            alpha = jnp.exp(m_prev - m_new)
            p = jnp.exp(s - m_new)
            l_scr[rows, :] = alpha * l_scr[rows, :] + jnp.sum(p, axis=-1, keepdims=True)
            m_scr[rows, :] = m_new
            p_scr[rows, :] = p.astype(BF16)
            acc_scr[rows, :] = acc_scr[rows, :] * alpha
        vb = v_ref[0, pl.ds(off, TK), 0:dv].astype(BF16)
        acc_scr[...] += jnp.dot(p_scr[...], vb, preferred_element_type=F32)
        return carry

    lax.fori_loop(0, own + 1, step, 0)

    def head_out(g):
        rows = slice(g * TQ, (g + 1) * TQ)
        return acc_scr[rows, :] / l_scr[rows, :]

    if variant == "A":
        lv = lam_ref[...]
        lam = (jnp.exp(jnp.sum(lv[0:1] * lv[1:2], axis=-1, keepdims=True))
               - jnp.exp(jnp.sum(lv[2:3] * lv[3:4], axis=-1, keepdims=True)) + lam_init)
        for h in range(N_HEADS):
            o = head_out(h) - lam * head_out(N_HEADS + h)
            o = o * lax.rsqrt(jnp.mean(o * o, axis=-1, keepdims=True) + LN_EPS)
            o_ref[0, :, h * LANES:(h + 1) * LANES] = (o * g_ref[...] * (1.0 - lam_init)).astype(o_ref.dtype)
    elif variant == "C":
        for pair in range(N_HEADS // 2):
            a0 = pltpu.roll(head_out(2 * pair), MOBA_DH, 1)
            a1 = head_out(2 * pair + 1)
            o_ref[0, :, pair * LANES:(pair + 1) * LANES] = jnp.where(lane < MOBA_DH, a0, a1).astype(o_ref.dtype)
    else:
        for h in range(N_HEADS):
            o_ref[0, :, h * dv:(h + 1) * dv] = head_out(h).astype(o_ref.dtype)


def _flash(variant, q3, kv3, tbl, *, q_blk, q_width, k_blk, k_width, v_blk=None, dv, out_width, groups,
           scale, mask=None, lam=None, subln_g=None, lam_init=0.0, n_sel=0, name):
    b, l, _ = kv3.shape
    rows = groups * TQ
    dq = k_width
    in_specs = [pl.BlockSpec((1, TQ, q_width), lambda bi, i: (bi, i, q_blk)),
                pl.BlockSpec((1, l, k_width), lambda bi, i: (bi, 0, k_blk))]
    args = [q3, kv3]
    if v_blk is not None:
        in_specs.append(pl.BlockSpec((1, l, dv), lambda bi, i: (bi, 0, v_blk)))
        args.append(kv3)
    in_specs.append(pl.BlockSpec(tbl.shape, lambda bi, i: (0, 0, 0)))
    args.append(tbl)
    if variant == "A":
        in_specs += [pl.BlockSpec(lam.shape, lambda bi, i: (0, 0)), pl.BlockSpec((1, LANES), lambda bi, i: (0, 0))]
        args += [lam, subln_g.reshape(1, LANES)]
    if variant == "B":
        in_specs.append(pl.BlockSpec((1, TQ, l), lambda bi, i: (bi, i, 0)))
        args.append(mask)
    scratch = [pltpu.VMEM((rows, dq), BF16), pltpu.VMEM((rows, TK), F32), pltpu.VMEM((rows, TK), BF16),
               pltpu.VMEM((rows, 1), F32), pltpu.VMEM((rows, 1), F32), pltpu.VMEM((rows, dv), F32)]
    if variant == "C":
        nblk = l // TK
        scratch += [pltpu.VMEM((nblk, rows, 1), F32), pltpu.VMEM((max(8, nblk), LANES), F32)]
    kern = functools.partial(_flash_kernel, variant=variant, scale=scale, n_sel=n_sel, lam_init=lam_init)
    return pl.pallas_call(
        kern, grid=(b, l // TQ), in_specs=in_specs,
        out_specs=pl.BlockSpec((1, TQ, out_width), lambda bi, i: (bi, i, 0)),
        out_shape=jax.ShapeDtypeStruct((b, l, out_width), BF16),
        scratch_shapes=scratch, compiler_params=_cp(("parallel", "arbitrary")), name=name)(*args)


def _page_copy(pool_ref, pt_ref, buf, sem, layer, seq, slot, row, page):
    return pltpu.make_async_copy(pool_ref.at[pt_ref[seq, page], layer], buf.at[slot, row], sem.at[slot])


def _fetch_pages(pool_ref, pt_ref, buf, sem, layer, n_pages, seqs=1):
    s = pl.program_id(0)
    slot = s % 2

    def start(step, into):
        def one_seq(b, c):
            for p in range(n_pages):
                _page_copy(pool_ref, pt_ref, buf, sem, layer, step * seqs + b, into, b * n_pages + p, p).start()
            return c
        lax.fori_loop(0, seqs, one_seq, 0)

    @pl.when(s == 0)
    def _():
        start(0, 0)

    @pl.when(s + 1 < pl.num_programs(0))
    def _():
        start(s + 1, 1 - slot)

    def wait_seq(b, c):
        for p in range(n_pages):
            _page_copy(pool_ref, pt_ref, buf, sem, layer, 0, slot, b * n_pages + p, p).wait()
        return c
    lax.fori_loop(0, seqs, wait_seq, 0)
    return slot


def _chunk_t(buf, slot, c):
    pages = [buf[slot, c * CHUNK_PAGES + p] for p in range(CHUNK_PAGES)]
    return jnp.concatenate(pages, axis=1).astype(BF16)


def _sample_attn_kernel(*refs, variant, layer, pieces, v_lo, v_hi, scale, n_sel, transposed):
    if variant == "B":
        pt_ref, q_ref, pool_ref, new_ref, tb_ref, ex_ref, o_ref, buf, sem, s_scr, p_scr = refs
    elif variant == "C":
        pt_ref, q_ref, pool_ref, new_ref, tb_ref, o_ref, buf, sem, s_scr, p_scr, mean_scr, e_scr = refs
    else:
        pt_ref, q_ref, pool_ref, new_ref, tb_ref, o_ref, buf, sem, s_scr, p_scr = refs
    n_pages = buf.shape[1]
    past = n_pages * PAGE
    ckeys = CHUNK_PAGES * PAGE
    n_chunks = n_pages // CHUNK_PAGES
    slot = _fetch_pages(pool_ref, pt_ref, buf, sem, layer, n_pages)

    qf = q_ref[0]
    qb = qf.astype(BF16)

    if variant == "C":
        nblk = past // MOBA_BLOCK

        @pl.when(pl.program_id(0) == 0)
        def _():
            blk = lax.broadcasted_iota(I32, e_scr.shape, 0)
            key = lax.broadcasted_iota(I32, e_scr.shape, 1)
            e_scr[...] = jnp.where(key // MOBA_BLOCK == blk, 1.0, 0.0).astype(BF16)

    for c in range(n_chunks):
        s = None
        if transposed:
            kt = _chunk_t(buf, slot, c)
            for lo, hi in pieces:
                part = jnp.dot(qb[:, lo:hi], kt[lo:hi, :], preferred_element_type=F32)
                s = part if s is None else s + part
        else:
            kc = buf[slot, c * CHUNK_PAGES:(c + 1) * CHUNK_PAGES].reshape(ckeys, buf.shape[3])
            for lo, hi in pieces:
                part = lax.dot_general(qb[:, lo:hi], kc[:, lo:hi].astype(BF16), (((1,), (1,)), ((), ())),
                                       preferred_element_type=F32)
                s = part if s is None else s + part
        s_scr[:, c * ckeys:(c + 1) * ckeys] = s
        if variant == "C":
            per = ckeys // MOBA_BLOCK
            for n in range(per):
                mean_scr[c * per + n:c * per + n + 1, :] = (
                    jnp.sum(kc[n * MOBA_BLOCK:(n + 1) * MOBA_BLOCK, :], axis=0, keepdims=True) * (1.0 / MOBA_BLOCK))

    s = s_scr[...] * scale
    new = new_ref[0]
    s_new = None
    for lo, hi in pieces:
        part = jnp.sum(qf[:, lo:hi] * new[:, lo:hi], axis=-1, keepdims=True)
        s_new = part if s_new is None else s_new + part
    s_new = s_new * scale
    if tb_ref is not None:
        s = s + tb_ref[:, 0:past]
        s_new = s_new + tb_ref[:, past:past + 1]
    if variant == "B":
        s = s + ex_ref[0, :, 0:past]
        s_new = s_new + ex_ref[0, :, past:past + 1]
    if variant == "C":
        gate = lax.dot_general(qf, mean_scr[...], (((1,), (1,)), ((), ())), preferred_element_type=F32)
        nlane = lax.broadcasted_iota(I32, gate.shape, 1)
        rank = jnp.zeros(gate.shape, F32)
        for n2 in range(nblk):
            col = gate[:, n2:n2 + 1]
            rank = rank + jnp.where(col > gate, 1.0,
                                    jnp.where(col == gate, jnp.where(n2 < nlane, 1.0, 0.0), 0.0))
        chosen = jnp.where(rank < n_sel, 1.0, 0.0).astype(BF16)
        keep = jnp.dot(chosen, e_scr[...], preferred_element_type=F32)
        s = s + (keep - 1.0) * (-NEG)

    m = jnp.maximum(jnp.max(s, axis=-1, keepdims=True), s_new)
    p = jnp.exp(s - m)
    p_new = jnp.exp(s_new - m)
    denom = jnp.sum(p, axis=-1, keepdims=True) + p_new
    p_scr[...] = p.astype(BF16)
    acc = p_new * new[:, v_lo:v_hi]
    for c in range(n_chunks):
        pc = p_scr[:, c * ckeys:(c + 1) * ckeys]
        if transposed:
            vt = _chunk_t(buf, slot, c)[v_lo:v_hi, :]
            acc = acc + lax.dot_general(pc, vt, (((1,), (1,)), ((), ())), preferred_element_type=F32)
        else:
            vc = buf[slot, c * CHUNK_PAGES:(c + 1) * CHUNK_PAGES].reshape(ckeys, buf.shape[3])[:, v_lo:v_hi]
            acc = acc + jnp.dot(pc, vc.astype(BF16), preferred_element_type=F32)
    o_ref[0] = acc / denom


def _sample_attn(variant, q, pool, layer, page_table, new_rows, tb, *, pieces, v_lo, v_hi, scale,
                 extra=None, n_sel=0, transposed=False, name):
    ns, _, dq = q.shape
    n_pages = page_table.shape[1]
    past = n_pages * PAGE
    w = new_rows.shape[2]
    page_shape = (w, PAGE) if transposed else (PAGE, w)
    in_specs = [pl.BlockSpec((1, SROWS, dq), lambda s, pt: (s, 0, 0)),
                pl.BlockSpec(memory_space=pl.ANY),
                pl.BlockSpec((1, 1, w), lambda s, pt: (s, 0, 0))]
    args = [q, pool, new_rows]
    if tb is not None:
        in_specs.append(pl.BlockSpec(tb.shape, lambda s, pt: (0, 0)))
        args.append(tb)
    if variant == "B":
        in_specs.append(pl.BlockSpec((1, 1, past + LANES), lambda s, pt: (s, 0, 0)))
        args.append(extra)
    scratch = [pltpu.VMEM((2, n_pages) + page_shape, F32), pltpu.SemaphoreType.DMA((2,)),
               pltpu.VMEM((SROWS, past), F32), pltpu.VMEM((SROWS, past), BF16)]
    if variant == "C":
        nblk = past // MOBA_BLOCK
        scratch += [pltpu.VMEM((nblk, w), F32), pltpu.VMEM((nblk, past), BF16)]
    kern = functools.partial(_sample_attn_kernel, variant=variant, layer=layer, pieces=pieces,
                             v_lo=v_lo, v_hi=v_hi, scale=scale, n_sel=n_sel, transposed=transposed)
    if tb is None:
        inner = kern

        def kern(pt_ref, q_ref, pool_ref, new_ref, *rest):
            return inner(pt_ref, q_ref, pool_ref, new_ref, None, *rest)

    return pl.pallas_call(
        kern,
        grid_spec=pltpu.PrefetchScalarGridSpec(
            num_scalar_prefetch=1, grid=(ns,), in_specs=in_specs,
            out_specs=pl.BlockSpec((1, SROWS, v_hi - v_lo), lambda s, pt: (s, 0, 0)),
            scratch_shapes=scratch),
        out_shape=jax.ShapeDtypeStruct((ns, SROWS, v_hi - v_lo), F32),
        compiler_params=_cp(("arbitrary",)), name=name)(page_table, *args)


IDX_SEQS = 8


def _idx_sample_kernel(pt_ref, iq_ref, w_ref, pool_ref, new_ref, o_ref, buf, sem, row_scr, key_scr, *,
                       layer, topk, n_pages):
    past = n_pages * PAGE
    ckeys = CHUNK_PAGES * PAGE
    slot = _fetch_pages(pool_ref, pt_ref, buf, sem, layer, n_pages, seqs=IDX_SEQS)
    lane = lax.broadcasted_iota(I32, (1, LANES), 1)
    for b in range(IDX_SEQS):
        iqf = iq_ref[b]
        iqb = iqf.astype(BF16)
        wh = w_ref[b][:, 0:1] * IDX_HEADS ** -0.5
        for c in range(n_pages // CHUNK_PAGES):
            pages = [buf[slot, b * n_pages + c * CHUNK_PAGES + p] for p in range(CHUNK_PAGES)]
            kt = jnp.concatenate(pages, axis=1).astype(BF16)
            sc = jnp.maximum(jnp.dot(iqb, kt, preferred_element_type=F32) * IDX_DH ** -0.5, 0.0)
            row_scr[b:b + 1, c * ckeys:(c + 1) * ckeys] = jnp.sum(sc * wh, axis=0, keepdims=True)
        sc_new = jnp.maximum(jnp.sum(iqf * new_ref[b], axis=-1, keepdims=True) * IDX_DH ** -0.5, 0.0)
        score_new = jnp.sum(sc_new * wh, axis=0, keepdims=True)
        row_scr[b:b + 1, past:past + LANES] = jnp.where(lane == 0, score_new, -3e38)
    o_ref[0] = _topk_additive(row_scr[...], key_scr, topk)


def _idx_sample(iq, iw, pool_t, layer, page_table, new_rows, topk):
    ns = iq.shape[0]
    n_pages = page_table.shape[1]
    past = n_pages * PAGE
    wide = past + LANES
    out = pl.pallas_call(
        functools.partial(_idx_sample_kernel, layer=layer, topk=topk, n_pages=n_pages),
        grid_spec=pltpu.PrefetchScalarGridSpec(
            num_scalar_prefetch=1, grid=(ns // IDX_SEQS,),
            in_specs=[pl.BlockSpec((IDX_SEQS, SROWS, IDX_DH), lambda s, pt: (s, 0, 0)),
                      pl.BlockSpec((IDX_SEQS, SROWS, LANES), lambda s, pt: (s, 0, 0)),
                      pl.BlockSpec(memory_space=pl.ANY),
                      pl.BlockSpec((IDX_SEQS, 1, IDX_DH), lambda s, pt: (s, 0, 0))],
            out_specs=pl.BlockSpec((1, IDX_SEQS, wide), lambda s, pt: (s, 0, 0)),
            scratch_shapes=[pltpu.VMEM((2, IDX_SEQS * n_pages, IDX_DH, PAGE), F32), pltpu.SemaphoreType.DMA((2,)),
                            pltpu.VMEM((IDX_SEQS, wide), F32), pltpu.VMEM((IDX_SEQS, wide), I32)]),
        out_shape=jax.ShapeDtypeStruct((ns // IDX_SEQS, IDX_SEQS, wide), F32),
        compiler_params=_cp(("arbitrary",)), name="dsa_index_sample")(page_table, iq, iw, pool_t, new_rows)
    return out.reshape(ns, 1, wide)


def _diff_post_kernel(o_ref, lam_ref, g_ref, out_ref, *, lam_init):
    lv = lam_ref[...]
    lam = (jnp.exp(jnp.sum(lv[0:1] * lv[1:2], axis=-1, keepdims=True))
           - jnp.exp(jnp.sum(lv[2:3] * lv[3:4], axis=-1, keepdims=True)) + lam_init)
    o = o_ref[...]
    d = o[:, 0:N_HEADS, :] - lam * o[:, N_HEADS:2 * N_HEADS, :]
    d = d * lax.rsqrt(jnp.mean(d * d, axis=-1, keepdims=True) + LN_EPS)
    out_ref[...] = d * g_ref[...] * (1.0 - lam_init)


def _diff_post(o, lam, g, lam_init):
    ns = o.shape[0]
    vm = pl.BlockSpec(memory_space=pltpu.VMEM)
    return pl.pallas_call(
        functools.partial(_diff_post_kernel, lam_init=lam_init), in_specs=[vm, vm, vm], out_specs=vm,
        out_shape=jax.ShapeDtypeStruct((ns, N_HEADS, LANES), F32),
        compiler_params=_cp(None), name="diff_post_sample")(o, lam, g.reshape(1, LANES))


def _rope_group(x, cos, sin):
    lane = lax.broadcasted_iota(I32, x.shape, 1)
    half = MLA_ROPE // 2
    other = jnp.where(lane < half, pltpu.roll(x, LANES - half, 1), pltpu.roll(x, half, 1))
    return x * cos + other * sin


def _mla_prep_kernel(h_ref, qg_ref, kg_ref, cos_ref, sin_ref, qn_ref, kv_ref):
    qa = h_ref[:, 0:MLA_Q_RANK]
    qn = qa * lax.rsqrt(jnp.mean(qa * qa, axis=-1, keepdims=True) + LN_EPS) * qg_ref[...]
    qn_ref[...] = qn.astype(BF16)
    kva = h_ref[:, MLA_Q_RANK:MLA_Q_RANK + MLA_KV_RANK]
    kv_ref[:, 0:MLA_KV_RANK] = kva * lax.rsqrt(jnp.mean(kva * kva, axis=-1, keepdims=True) + LN_EPS) * kg_ref[...]
    kr = h_ref[:, MLA_Q_RANK + MLA_KV_RANK + LANES:MLA_Q_RANK + MLA_KV_RANK + 2 * LANES]
    kv_ref[:, MLA_KV_RANK:MLA_KV_RANK + LANES] = _rope_group(kr, cos_ref[...], sin_ref[...])


def _mla_prep(h, qg, kg, cos, sin):
    m = h.shape[0]
    tm = _tile(m, 640, 16)
    wide = MLA_Q_RANK + MLA_KV_RANK + 2 * LANES
    return pl.pallas_call(
        _mla_prep_kernel, grid=(m // tm,),
        in_specs=[pl.BlockSpec((tm, wide), lambda i: (i, 1)),
                  pl.BlockSpec((1, MLA_Q_RANK), lambda i: (0, 0)), pl.BlockSpec((1, MLA_KV_RANK), lambda i: (0, 0)),
                  pl.BlockSpec((tm, LANES), lambda i: (i, 0)), pl.BlockSpec((tm, LANES), lambda i: (i, 0))],
        out_specs=[pl.BlockSpec((tm, MLA_Q_RANK), lambda i: (i, 0)),
                   pl.BlockSpec((tm, MLA_KV_RANK + LANES), lambda i: (i, 0))],
        out_shape=[jax.ShapeDtypeStruct((m, MLA_Q_RANK), BF16),
                   jax.ShapeDtypeStruct((m, MLA_KV_RANK + LANES), F32)],
        compiler_params=_cp(("parallel",)), name="mla_prep")(
            h, qg.reshape(1, -1), kg.reshape(1, -1), cos, sin)


def _mla_q_kernel(qf_ref, wuk_ref, cos_ref, sin_ref, o_ref):
    width = MLA_KV_RANK + LANES
    for h in range(N_HEADS):
        nope = qf_ref[:, h * LANES:(h + 1) * LANES].astype(BF16)
        o_ref[:, h * width:h * width + MLA_KV_RANK] = jnp.dot(
            nope, wuk_ref[h], preferred_element_type=F32).astype(o_ref.dtype)
        rope = qf_ref[:, (N_HEADS + h) * LANES:(N_HEADS + h + 1) * LANES]
        o_ref[:, h * width + MLA_KV_RANK:(h + 1) * width] = _rope_group(
            rope, cos_ref[...], sin_ref[...]).astype(o_ref.dtype)


def _mla_q(qf, wuk_t, cos, sin, out_dtype):
    m = qf.shape[0]
    tm = _tile(m, 640, 16)
    width = N_HEADS * (MLA_KV_RANK + LANES)
    return pl.pallas_call(
        _mla_q_kernel, grid=(m // tm,),
        in_specs=[pl.BlockSpec((tm, qf.shape[1]), lambda i: (i, 0)),
                  pl.BlockSpec(wuk_t.shape, lambda i: (0, 0, 0)),
                  pl.BlockSpec((tm, LANES), lambda i: (i, 0)), pl.BlockSpec((tm, LANES), lambda i: (i, 0))],
        out_specs=pl.BlockSpec((tm, width), lambda i: (i, 0)),
        out_shape=jax.ShapeDtypeStruct((m, width), out_dtype),
        compiler_params=_cp(("parallel",)), name="mla_q")(qf, wuk_t, cos, sin)


def _mla_uv_kernel(o_ref, w_ref, d_ref):
    for h in range(N_HEADS):
        d_ref[:, h * MLA_VDH:(h + 1) * MLA_VDH] = jnp.dot(
            o_ref[:, h * MLA_KV_RANK:(h + 1) * MLA_KV_RANK], w_ref[h], preferred_element_type=F32).astype(BF16)


def _mla_uv(o_lat, wuv):
    m = o_lat.shape[0]
    tm = _tile(m, 640, 16)
    return pl.pallas_call(
        _mla_uv_kernel, grid=(m // tm,),
        in_specs=[pl.BlockSpec((tm, o_lat.shape[1]), lambda i: (i, 0)), pl.BlockSpec(wuv.shape, lambda i: (0, 0, 0))],
        out_specs=pl.BlockSpec((tm, N_HEADS * MLA_VDH), lambda i: (i, 0)),
        out_shape=jax.ShapeDtypeStruct((m, N_HEADS * MLA_VDH), BF16),
        compiler_params=_cp(("parallel",)), name="mla_uv")(o_lat, wuv)


def _prep_w_in_even(w):
    d = w.shape[0]
    pad = jnp.zeros((d, 3200 - 3144), w.dtype)
    cols = [w[:, 0:1024], w[:, 1280:2304], w[:, 2560:3072], w[:, 1024:1280], w[:, 2304:2560], w[:, 3072:3144], pad]
    return jnp.concatenate(cols, axis=1).astype(BF16)


def _prep_w_in_odd(w):
    d = w.shape[0]
    mq = jnp.pad(w[:, 0:512].reshape(d, N_HEADS, MOBA_DH), ((0, 0), (0, 0), (0, LANES - MOBA_DH))).reshape(d, -1)
    kr = jnp.pad(w[:, 1408:1472], ((0, 0), (0, LANES - MLA_ROPE)))
    return jnp.concatenate([mq, w[:, 640:1152], w[:, 1152:1408], w[:, 512:640], kr], axis=1).astype(BF16)


def _prep_w_q_up(w):
    r = w.shape[0]
    w3 = w.reshape(r, N_HEADS, MLA_NOPE + MLA_ROPE)
    nope = w3[:, :, :MLA_NOPE].reshape(r, -1)
    rope = jnp.pad(w3[:, :, MLA_NOPE:], ((0, 0), (0, 0), (0, LANES - MLA_ROPE))).reshape(r, -1)
    return jnp.concatenate([nope, rope], axis=1).astype(BF16)


def _rope_tables(pos):
    half = MLA_ROPE // 2
    freq = jnp.power(ROPE_THETA, -jnp.arange(half, dtype=F32) / half)
    ang = pos.astype(F32)[:, None] * freq
    cos, sin = jnp.cos(ang), jnp.sin(ang)
    zero = jnp.zeros((pos.shape[0], LANES - MLA_ROPE), F32)
    return jnp.concatenate([cos, cos, zero], axis=1), jnp.concatenate([-sin, sin, zero], axis=1)


def _sample_rows(x, n_rows=SROWS):
    return jnp.pad(x, ((0, 0), (0, n_rows - x.shape[1]), (0, 0)))


def kernel(x_prompt, x_sample, cache_diff_kv, cache_dsa_kv, cache_dsa_idx, cache_moba_kv, cache_mla, page_table, p_prompt, p_sample, rel_bias, w_in_even, diff_lambda, diff_subln_g, w_out_even, ffn_w_in, ffn_w_out, w_in_odd, mla_q_norm_g, mla_kv_norm_g, mla_w_q_up, mla_w_uk, mla_w_uv, w_out_odd, moe_router, moe_w_in, moe_w_out, ln1_g, ln1_b, ln2_g, ln2_b, ple_w_gate, ple_w_proj):
    b, l, d = x_prompt.shape
    ns, dec_seq, _ = x_sample.shape
    depth = ln1_g.shape[0]
    n_pages = page_table.shape[1]
    past = n_pages * PAGE
    n_prompt = b * l
    assert dec_seq == 1 and past % MOBA_BLOCK == 0 and n_pages % CHUNK_PAGES == 0
    assert l % TK == 0 and TK == MOBA_BLOCK and TK == 2 * TQ
    alpha = (2 * depth) ** 0.25

    x = jnp.concatenate([x_prompt.reshape(n_prompt, d), x_sample.reshape(ns, d)], axis=0)
    xb = x.astype(BF16)
    ple = jnp.concatenate([p_prompt.reshape(depth, n_prompt, -1), p_sample.reshape(depth, ns, -1)], axis=1).astype(BF16)

    tbl8 = _bias_tables_prompt(rel_bias)
    tbl0 = _bias_tables_prompt(jnp.zeros((REL_BUCKETS, 1), F32))
    tbs = _bias_table_sample(rel_bias, past)
    pos = jnp.concatenate([jnp.tile(jnp.arange(l), b), jnp.full((ns,), past)])
    cos, sin = _rope_tables(pos)
    lane = jnp.arange(LANES)
    idx_pool_t = jnp.swapaxes(cache_dsa_idx, 2, 3)
    mla_pool_t = jnp.swapaxes(cache_mla, 2, 3)
    n_tiles = (2 * (n_prompt + ns) + N_EXPERTS * (MOE_TILE - 1)) // MOE_TILE

    rows_diff, rows_dsa, rows_idx, rows_moba, rows_mla = [], [], [], [], []
    for i in range(depth):
        j = i // 2
        if i % 2 == 0:
            lam_init = 0.8 - 0.6 * math.exp(-0.3 * i)
            h = _matmul(xb, _prep_w_in_even(w_in_even[j]), F32, "in_proj_even", tn_pref=640)
            r_diff, r_dsa, r_idx = h[:, 2560:2816], h[:, 2816:3072], h[:, 3072:3072 + IDX_DH]
            rows_diff.append(r_diff); rows_dsa.append(r_dsa); rows_idx.append(r_idx)
            hp = h[:n_prompt].reshape(b, l, -1)
            hs = h[n_prompt:]
            a_p = _flash("A", hp, hp, tbl8, q_blk=0, q_width=1024, k_blk=20, k_width=LANES, v_blk=21, dv=LANES,
                         out_width=1024, groups=2 * N_HEADS, scale=DIFF_DH ** -0.5, lam=diff_lambda[j],
                         subln_g=diff_subln_g[j], lam_init=lam_init, name="diff_attn_prompt")
            topk = min(DSA_TOPK, l // 4)
            sel = _idx_prompt(hp, topk)
            s_p = _flash("B", hp, hp, tbl8, q_blk=1, q_width=1024, k_blk=22, k_width=LANES, v_blk=23, dv=LANES,
                         out_width=1024, groups=N_HEADS, scale=DSA_DH ** -0.5, mask=sel, name="dsa_attn_prompt")
            dq = hs[:, 0:1024].reshape(ns, N_HEADS, LANES)
            q_a = jnp.concatenate([jnp.where(lane < DIFF_DH, dq, 0.0), jnp.where(lane >= DIFF_DH, dq, 0.0)], axis=1)
            o_a = _sample_attn("A", q_a, cache_diff_kv, j, page_table, hs[:, None, 2560:2816], tbs,
                               pieces=((0, LANES),), v_lo=LANES, v_hi=2 * LANES, scale=DIFF_DH ** -0.5,
                               name="diff_attn_sample")
            a_s = _diff_post(o_a, diff_lambda[j], diff_subln_g[j], lam_init).reshape(ns, -1)
            iq = _sample_rows(hs[:, 2048:2560].reshape(ns, IDX_HEADS, IDX_DH))
            iw = _sample_rows(jnp.broadcast_to(hs[:, 3072 + IDX_DH:3072 + IDX_DH + IDX_HEADS, None],
                                               (ns, IDX_HEADS, LANES)))
            topk_s = min(DSA_TOPK, (past + 1) // 4)
            sel_s = _idx_sample(iq, iw, idx_pool_t, j, page_table, hs[:, None, 3072:3072 + IDX_DH], topk_s)
            q_s = _sample_rows(hs[:, 1024:2048].reshape(ns, N_HEADS, LANES))
            o_s = _sample_attn("B", q_s, cache_dsa_kv, j, page_table, hs[:, None, 2816:3072], tbs,
                               pieces=((0, LANES),), v_lo=LANES, v_hi=2 * LANES, scale=DSA_DH ** -0.5,
                               extra=sel_s, name="dsa_attn_sample")
            s_s = o_s[:, :N_HEADS].reshape(ns, -1)
            mix_p = jnp.concatenate([a_p.reshape(n_prompt, -1), s_p.reshape(n_prompt, -1)], axis=1)
            mix_s = jnp.concatenate([a_s, s_s], axis=1).astype(BF16)
            mix = jnp.concatenate([mix_p, mix_s], axis=0)
            hout = _matmul(mix, w_out_even[j].astype(BF16), F32, "out_proj_even")
        else:
            h = _matmul(xb, _prep_w_in_odd(w_in_odd[j]), F32, "in_proj_odd")
            r_moba = h[:, 1792:1920]
            rows_moba.append(r_moba)
            qn, kvrow = _mla_prep(h, mla_q_norm_g[j], mla_kv_norm_g[j], cos, sin)
            rows_mla.append(kvrow[:, :MLA_KV_RANK + MLA_ROPE])
            qf = _matmul(qn, _prep_w_q_up(mla_w_q_up[j]), F32, "mla_q_up")
            wuk_t = jnp.transpose(mla_w_uk[j], (1, 2, 0)).astype(BF16)
            wuv = jnp.transpose(mla_w_uv[j], (1, 0, 2)).astype(BF16)
            qmla = _mla_q(qf, wuk_t, cos, sin, F32)
            hp = h[:n_prompt].reshape(b, l, -1)
            hs = h[n_prompt:]
            m_p = _flash("C", hp, hp, tbl8, q_blk=0, q_width=1024, k_blk=14, k_width=LANES, dv=LANES,
                         out_width=N_HEADS * MOBA_DH, groups=N_HEADS, scale=MOBA_DH ** -0.5,
                         n_sel=min(MOBA_TOPK, l // MOBA_BLOCK), name="moba_attn_prompt")
            width = MLA_KV_RANK + LANES
            o_lat_p = _flash("D", qmla[:n_prompt].astype(BF16).reshape(b, l, -1), kvrow[:n_prompt].reshape(b, l, -1),
                             tbl0, q_blk=0, q_width=N_HEADS * width, k_blk=0, k_width=width, dv=MLA_KV_RANK,
                             out_width=N_HEADS * MLA_KV_RANK, groups=N_HEADS,
                             scale=(MLA_NOPE + MLA_ROPE) ** -0.5, name="mla_attn_prompt")
            q_c = _sample_rows(hs[:, 0:1024].reshape(ns, N_HEADS, LANES))
            o_c = _sample_attn("C", q_c, cache_moba_kv, j, page_table, hs[:, None, 1792:1920], tbs,
                               pieces=((0, LANES),), v_lo=0, v_hi=LANES, scale=MOBA_DH ** -0.5,
                               n_sel=min(MOBA_TOPK, (past + 1) // MOBA_BLOCK), name="moba_attn_sample")
            m_s = o_c[:, :N_HEADS, MOBA_DH:].reshape(ns, -1)
            q_d = _sample_rows(qmla[n_prompt:].reshape(ns, N_HEADS, width)[:, :, :MLA_KV_RANK + MLA_ROPE])
            o_d = _sample_attn("D", q_d, mla_pool_t, j, page_table, kvrow[n_prompt:, None, :MLA_KV_RANK + MLA_ROPE],
                               None, pieces=((0, MLA_KV_RANK), (MLA_KV_RANK, MLA_KV_RANK + MLA_ROPE)),
                               v_lo=0, v_hi=MLA_KV_RANK, scale=(MLA_NOPE + MLA_ROPE) ** -0.5, transposed=True,
                               name="mla_attn_sample")
            o_lat = jnp.concatenate([o_lat_p.reshape(n_prompt, -1),
                                     o_d[:, :N_HEADS].reshape(ns, -1).astype(BF16)], axis=0)
            dmla = _mla_uv(o_lat, wuv)
            m_all = jnp.concatenate([m_p.reshape(n_prompt, -1), m_s.astype(BF16)], axis=0)
            mix = jnp.concatenate([m_all, dmla], axis=1)
            hout = _matmul(mix, w_out_odd[j].astype(BF16), F32, "out_proj_odd")

        x, xb = _deepnorm_ln(x, hout, ln1_g[i], ln1_b[i], alpha, "deepnorm_ln1")
        if i % 2 == 0:
            mid = _swiglu(xb, ffn_w_in[j][None].astype(BF16), "ffn_in")
            f = _matmul_k(mid, ffn_w_out[j].astype(BF16), "ffn_out")
            x, xb = _deepnorm_ln(x, f, ln2_g[i], ln2_b[i], alpha, "deepnorm_ln2")
        else:
            router = jnp.pad(moe_router[j], ((0, 0), (0, LANES - N_EXPERTS))).astype(BF16)
            route = _router(xb, router, "moe_router")
            slot1, slot2, token_of_slot, tile_expert, n_used = _dispatch_plan(route, n_tiles)
            xs = _gather_rows(x, token_of_slot)
            mid = _grouped_swiglu(xs, moe_w_in[j], tile_expert, n_used)
            ys = _grouped_matmul_k(mid, moe_w_out[j].astype(BF16), tile_expert, n_used)
            x, xb = _moe_combine_ln(ys, slot1, slot2, route, x, ln2_g[i], ln2_b[i], alpha)
        x, xb = _ple(xb, x, ple_w_gate[i].astype(BF16), ple[i], ple_w_proj[i].astype(BF16), "ple_gate")

    def split(rows, width):
        r = jnp.stack(rows, axis=0)
        rp = jnp.transpose(r[:, :n_prompt].reshape(-1, b, l, width), (1, 0, 2, 3))
        rs = jnp.transpose(r[:, n_prompt:].reshape(-1, ns, 1, width), (1, 0, 2, 3))
        return rp, rs

    dp, ds = split(rows_diff, 256)
    sp, ss = split(rows_dsa, 256)
    ip, is_ = split(rows_idx, IDX_DH)
    mp, ms = split(rows_moba, 2 * MOBA_DH)
    lp, ls = split(rows_mla, MLA_KV_RANK + MLA_ROPE)
    return (x[:n_prompt].reshape(b, l, d), x[n_prompt:].reshape(ns, 1, d), dp, ds, sp, ss, ip, is_, mp, ms, lp, ls)
```

```python
import functools
import math

import jax
import jax.numpy as jnp
from jax import lax
from jax.experimental import pallas as pl
from jax.experimental.pallas import tpu as pltpu

F32, BF16, I32 = jnp.float32, jnp.bfloat16, jnp.int32

N_HEADS = 8
DIFF_DH = 64
DSA_DH = 128
IDX_HEADS = 8
IDX_DH = 64
DSA_TOPK = 256
MOBA_DH = 64
MOBA_BLOCK = 256
MOBA_TOPK = 3
MLA_Q_RANK = 512
MLA_KV_RANK = 256
MLA_NOPE = 128
MLA_ROPE = 64
MLA_VDH = 128
ROPE_THETA = 10000.0
REL_BUCKETS = 32
REL_MAX_DIST = 128
N_EXPERTS = 8
PAGE = 128
NEG = -1e30
LN_EPS = 1e-5
INT_MIN = -(2 ** 31)

LANES = 128
TQ = 128
TK = 256
SROWS = 16
CHUNK_PAGES = 16
VMEM_LIMIT = 52 * 1024 * 1024


def _cp(sem, vmem=VMEM_LIMIT):
    return pltpu.CompilerParams(dimension_semantics=sem, vmem_limit_bytes=vmem)


def _tile(n, pref, mult):
    best = None
    for t in range(mult, min(n, pref) + 1, mult):
        if n % t == 0:
            best = t
    return best if best is not None else n


def _mm_kernel(x_ref, w_ref, o_ref):
    o_ref[...] = jnp.dot(x_ref[...], w_ref[...], preferred_element_type=F32).astype(o_ref.dtype)


def _matmul(x, w, out_dtype, name, tn_pref=512):
    m, k = x.shape
    n = w.shape[1]
    tm = _tile(m, 640, 16)
    tn = _tile(n, tn_pref, LANES)
    return pl.pallas_call(
        _mm_kernel, grid=(m // tm, n // tn),
        in_specs=[pl.BlockSpec((tm, k), lambda i, j: (i, 0)), pl.BlockSpec((k, tn), lambda i, j: (0, j))],
        out_specs=pl.BlockSpec((tm, tn), lambda i, j: (i, j)),
        out_shape=jax.ShapeDtypeStruct((m, n), out_dtype),
        compiler_params=_cp(("parallel", "arbitrary")), name=name)(x, w)


def _swiglu_kernel(x_ref, wg_ref, wu_ref, o_ref):
    x = x_ref[...]
    g = jnp.dot(x, wg_ref[...], preferred_element_type=F32)
    u = jnp.dot(x, wu_ref[...], preferred_element_type=F32)
    o_ref[...] = (g * jax.nn.sigmoid(g) * u).astype(o_ref.dtype)


def _swiglu(x, w_in, name):
    m, k = x.shape
    e, _, f2 = w_in.shape
    f = f2 // 2
    tm = _tile(m, 640, 16)
    tn = _tile(f, 512, LANES)
    nper = f // tn
    return pl.pallas_call(
        _swiglu_kernel, grid=(m // tm, e * nper),
        in_specs=[pl.BlockSpec((tm, k), lambda i, j: (i, 0)),
                  pl.BlockSpec((None, k, tn), lambda i, j: (j // nper, 0, j % nper)),
                  pl.BlockSpec((None, k, tn), lambda i, j: (j // nper, 0, j % nper + nper))],
        out_specs=pl.BlockSpec((tm, tn), lambda i, j: (i, j)),
        out_shape=jax.ShapeDtypeStruct((m, e * f), BF16),
        compiler_params=_cp(("parallel", "arbitrary")), name=name)(x, w_in, w_in)


def _ln_kernel(x_ref, h_ref, g_ref, b_ref, o_ref, ob_ref, *, alpha):
    z = alpha * x_ref[...] + h_ref[...]
    mu = jnp.mean(z, axis=-1, keepdims=True)
    zc = z - mu
    var = jnp.mean(zc * zc, axis=-1, keepdims=True)
    y = zc * lax.rsqrt(var + LN_EPS) * g_ref[...] + b_ref[...]
    o_ref[...] = y
    ob_ref[...] = y.astype(BF16)


def _deepnorm_ln(x, h, g, b, alpha, name):
    m, d = x.shape
    tm = _tile(m, 320, 16)
    row = pl.BlockSpec((tm, d), lambda i: (i, 0))
    vec = pl.BlockSpec((1, d), lambda i: (0, 0))
    return pl.pallas_call(
        functools.partial(_ln_kernel, alpha=alpha), grid=(m // tm,),
        in_specs=[row, row, vec, vec], out_specs=[row, row],
        out_shape=[jax.ShapeDtypeStruct((m, d), F32), jax.ShapeDtypeStruct((m, d), BF16)],
        compiler_params=_cp(("parallel",)), name=name)(x, h, g.reshape(1, d), b.reshape(1, d))


def _ple_kernel(xb_ref, x_ref, wg_ref, pb_ref, wp_ref, o_ref, ob_ref):
    gate = jax.nn.sigmoid(jnp.dot(xb_ref[...], wg_ref[...], preferred_element_type=F32))
    proj = jnp.dot(pb_ref[...], wp_ref[...], preferred_element_type=F32)
    y = x_ref[...] + gate * proj
    o_ref[...] = y
    ob_ref[...] = y.astype(BF16)


def _ple(xb, x, wg, pb, wp, name):
    m, d = x.shape
    pd = pb.shape[1]
    tm = _tile(m, 640, 16)
    tn = _tile(d, 512, LANES)
    return pl.pallas_call(
        _ple_kernel, grid=(m // tm, d // tn),
        in_specs=[pl.BlockSpec((tm, d), lambda i, j: (i, 0)), pl.BlockSpec((tm, tn), lambda i, j: (i, j)),
                  pl.BlockSpec((d, tn), lambda i, j: (0, j)), pl.BlockSpec((tm, pd), lambda i, j: (i, 0)),
                  pl.BlockSpec((pd, tn), lambda i, j: (0, j))],
        out_specs=[pl.BlockSpec((tm, tn), lambda i, j: (i, j)), pl.BlockSpec((tm, tn), lambda i, j: (i, j))],
        out_shape=[jax.ShapeDtypeStruct((m, d), F32), jax.ShapeDtypeStruct((m, d), BF16)],
        compiler_params=_cp(("parallel", "arbitrary")), name=name)(xb, x, wg, pb, wp)


def _router_kernel(xb_ref, w_ref, o_ref):
    logits = jnp.dot(xb_ref[...], w_ref[...], preferred_element_type=F32)
    lane = lax.broadcasted_iota(I32, logits.shape, 1)
    logits = jnp.where(lane < N_EXPERTS, logits, -jnp.inf)
    m1 = jnp.max(logits, axis=-1, keepdims=True)
    i1 = jnp.min(jnp.where(logits == m1, lane, LANES), axis=-1, keepdims=True)
    rest = jnp.where(lane == i1, -jnp.inf, logits)
    m2 = jnp.max(rest, axis=-1, keepdims=True)
    i2 = jnp.min(jnp.where(rest == m2, lane, LANES), axis=-1, keepdims=True)
    e2 = jnp.exp(m2 - m1)
    den = 1.0 + e2
    o_ref[...] = jnp.where(lane == 0, i1.astype(F32),
                           jnp.where(lane == 1, i2.astype(F32),
                                     jnp.where(lane == 2, 1.0 / den, jnp.where(lane == 3, e2 / den, 0.0))))


def _router(xb, w, name):
    m, d = xb.shape
    tm = _tile(m, 640, 16)
    return pl.pallas_call(
        _router_kernel, grid=(m // tm,),
        in_specs=[pl.BlockSpec((tm, d), lambda i: (i, 0)), pl.BlockSpec((d, LANES), lambda i: (0, 0))],
        out_specs=pl.BlockSpec((tm, LANES), lambda i: (i, 0)),
        out_shape=jax.ShapeDtypeStruct((m, LANES), F32),
        compiler_params=_cp(("parallel",)), name=name)(xb, w)


MOE_TILE = 512
GATHER_ROWS = 256


def _row_copy(src_hbm, src_row, dst, dst_row, sem):
    return pltpu.make_async_copy(src_hbm.at[pl.ds(src_row, 1)], dst.at[pl.ds(dst_row, 1)], sem)


def _gather_rows_kernel(tok_ref, x_hbm, o_ref, buf, sem):
    base = pl.program_id(0) * buf.shape[0]

    def issue(r, c):
        _row_copy(x_hbm, tok_ref[base + r], buf, r, sem).start()
        return c

    def drain(r, c):
        _row_copy(x_hbm, 0, buf, r, sem).wait()
        return c

    lax.fori_loop(0, buf.shape[0], issue, 0)
    lax.fori_loop(0, buf.shape[0], drain, 0)
    o_ref[...] = buf[...].astype(o_ref.dtype)


def _gather_rows(x, token_of_slot):
    p = token_of_slot.shape[0]
    d = x.shape[1]
    return pl.pallas_call(
        _gather_rows_kernel,
        grid_spec=pltpu.PrefetchScalarGridSpec(
            num_scalar_prefetch=1, grid=(p // GATHER_ROWS,),
            in_specs=[pl.BlockSpec(memory_space=pl.ANY)],
            out_specs=pl.BlockSpec((GATHER_ROWS, d), lambda i, tok: (i, 0)),
            scratch_shapes=[pltpu.VMEM((GATHER_ROWS, d), x.dtype), pltpu.SemaphoreType.DMA(())]),
        out_shape=jax.ShapeDtypeStruct((p, d), BF16),
        compiler_params=_cp(("arbitrary",)), name="moe_gather")(token_of_slot, x)


def _gswiglu_kernel(te_ref, nu_ref, x_ref, wg_ref, wu_ref, o_ref, wg_scr, wu_scr):
    i = pl.program_id(1)

    @pl.when(jnp.logical_or(i == 0, te_ref[i] != te_ref[jnp.maximum(i - 1, 0)]))
    def _():
        wg_scr[...] = wg_ref[...].astype(BF16)
        wu_scr[...] = wu_ref[...].astype(BF16)

    @pl.when(i < nu_ref[0])
    def _():
        x = x_ref[...]
        g = jnp.dot(x, wg_scr[...], preferred_element_type=F32)
        u = jnp.dot(x, wu_scr[...], preferred_element_type=F32)
        o_ref[...] = (g * jax.nn.sigmoid(g) * u).astype(o_ref.dtype)

    @pl.when(i >= nu_ref[0])
    def _():
        o_ref[...] = jnp.zeros_like(o_ref)


def _grouped_swiglu(xs, w_in, layer, tile_expert, n_used):
    p, k = xs.shape
    f = w_in.shape[3] // 2
    tn = _tile(f, 512, LANES)
    nper = f // tn
    return pl.pallas_call(
        _gswiglu_kernel,
        grid_spec=pltpu.PrefetchScalarGridSpec(
            num_scalar_prefetch=2, grid=(nper, p // MOE_TILE),
            in_specs=[pl.BlockSpec((MOE_TILE, k), lambda j, i, te, nu: (i, 0)),
                      pl.BlockSpec((None, None, k, tn), lambda j, i, te, nu: (layer, te[i], 0, j)),
                      pl.BlockSpec((None, None, k, tn), lambda j, i, te, nu: (layer, te[i], 0, j + nper))],
            out_specs=pl.BlockSpec((MOE_TILE, tn), lambda j, i, te, nu: (i, j)),
            scratch_shapes=[pltpu.VMEM((k, tn), BF16), pltpu.VMEM((k, tn), BF16)]),
        out_shape=jax.ShapeDtypeStruct((p, f), BF16),
        compiler_params=_cp(("arbitrary", "arbitrary")), name="moe_in")(tile_expert, n_used, xs, w_in, w_in)


def _gmm_kernel(te_ref, nu_ref, h_ref, w_ref, o_ref):
    @pl.when(pl.program_id(1) < nu_ref[0])
    def _():
        o_ref[...] = jnp.dot(h_ref[...], w_ref[...], preferred_element_type=F32)

    @pl.when(pl.program_id(1) >= nu_ref[0])
    def _():
        o_ref[...] = jnp.zeros_like(o_ref)


def _grouped_matmul(h, w, layer, tile_expert, n_used):
    p, f = h.shape
    n = w.shape[3]
    tn = _tile(n, 512, LANES)
    return pl.pallas_call(
        _gmm_kernel,
        grid_spec=pltpu.PrefetchScalarGridSpec(
            num_scalar_prefetch=2, grid=(n // tn, p // MOE_TILE),
            in_specs=[pl.BlockSpec((MOE_TILE, f), lambda j, i, te, nu: (i, 0)),
                      pl.BlockSpec((None, None, f, tn), lambda j, i, te, nu: (layer, te[i], 0, j))],
            out_specs=pl.BlockSpec((MOE_TILE, tn), lambda j, i, te, nu: (i, j))),
        out_shape=jax.ShapeDtypeStruct((p, n), F32),
        compiler_params=_cp(("arbitrary", "arbitrary")), name="moe_out")(tile_expert, n_used, h, w)


def _moe_combine_ln_kernel(s1_ref, s2_ref, route_ref, x_ref, ys_hbm, g_ref, b_ref, o_ref, ob_ref,
                           buf1, buf2, sem, *, alpha):
    tm = buf1.shape[0]
    base = pl.program_id(0) * tm

    def issue(r, c):
        _row_copy(ys_hbm, s1_ref[base + r], buf1, r, sem).start()
        _row_copy(ys_hbm, s2_ref[base + r], buf2, r, sem).start()
        return c

    def drain(r, c):
        _row_copy(ys_hbm, 0, buf1, r, sem).wait()
        _row_copy(ys_hbm, 0, buf2, r, sem).wait()
        return c

    lax.fori_loop(0, tm, issue, 0)
    lax.fori_loop(0, tm, drain, 0)
    f = route_ref[:, 2:3] * buf1[...] + route_ref[:, 3:4] * buf2[...]
    z = alpha * x_ref[...] + f
    mu = jnp.mean(z, axis=-1, keepdims=True)
    zc = z - mu
    var = jnp.mean(zc * zc, axis=-1, keepdims=True)
    y = zc * lax.rsqrt(var + LN_EPS) * g_ref[...] + b_ref[...]
    o_ref[...] = y
    ob_ref[...] = y.astype(BF16)


def _moe_combine_ln(ys, slot1, slot2, route, x, g, b, alpha):
    m, d = x.shape
    tm = _tile(m, 320, 16)
    row = lambda i, s1, s2: (i, 0)
    vec = pl.BlockSpec((1, d), lambda i, s1, s2: (0, 0))
    return pl.pallas_call(
        functools.partial(_moe_combine_ln_kernel, alpha=alpha),
        grid_spec=pltpu.PrefetchScalarGridSpec(
            num_scalar_prefetch=2, grid=(m // tm,),
            in_specs=[pl.BlockSpec((tm, LANES), row), pl.BlockSpec((tm, d), row),
                      pl.BlockSpec(memory_space=pl.ANY), vec, vec],
            out_specs=[pl.BlockSpec((tm, d), row), pl.BlockSpec((tm, d), row)],
            scratch_shapes=[pltpu.VMEM((tm, d), F32), pltpu.VMEM((tm, d), F32), pltpu.SemaphoreType.DMA(())]),
        out_shape=[jax.ShapeDtypeStruct((m, d), F32), jax.ShapeDtypeStruct((m, d), BF16)],
        compiler_params=_cp(("arbitrary",)), name="moe_combine_ln")(
            slot1, slot2, route, x, ys, g.reshape(1, d), b.reshape(1, d))


def _dispatch_plan(route, n_tiles):
    m = route.shape[0]
    e = jnp.concatenate([route[:, 0], route[:, 1]]).astype(I32)
    onehot = (e[:, None] == jnp.arange(N_EXPERTS)[None, :]).astype(I32)
    pos = jnp.sum((jnp.cumsum(onehot, axis=0) - 1) * onehot, axis=1)
    counts = jnp.sum(onehot, axis=0)
    padded = (counts + MOE_TILE - 1) // MOE_TILE * MOE_TILE
    ends = jnp.cumsum(padded)
    slot = (ends - padded)[e] + pos
    token = jnp.tile(jnp.arange(m, dtype=I32), 2)
    token_of_slot = jnp.zeros((n_tiles * MOE_TILE,), I32).at[slot].set(token)
    n_used = (ends[-1] // MOE_TILE).astype(I32)
    tile_start = jnp.arange(n_tiles, dtype=I32) * MOE_TILE
    tile_expert = jnp.sum((tile_start[:, None] >= ends[None, :]).astype(I32), axis=1)
    last = jnp.sum((ends[-1] - 1 >= ends).astype(I32))
    tile_expert = jnp.minimum(tile_expert, last).astype(I32)
    return slot[:m].astype(I32), slot[m:].astype(I32), token_of_slot, tile_expert, n_used.reshape(1)


def _t5_bucket(d):
    n = jnp.maximum(d, 0)
    max_exact = REL_BUCKETS // 2
    nf = jnp.maximum(n, 1).astype(F32)
    large = max_exact + (jnp.log(nf / max_exact) / math.log(REL_MAX_DIST / max_exact)
                         * (REL_BUCKETS - max_exact)).astype(I32)
    large = jnp.minimum(large, REL_BUCKETS - 1)
    return jnp.where(n < max_exact, n, large)


def _bias_rows(bucket, rb_ref, h):
    acc = jnp.zeros(bucket.shape, F32)
    for b in range(REL_BUCKETS):
        acc = jnp.where(bucket == b, rb_ref[b, h], acc)
    return acc


def _tbl_prompt_kernel(rb_ref, o_ref, *, hb):
    t = pl.program_id(0)
    off = (t // 2) * TK + (t % 2) * TQ
    k = lax.broadcasted_iota(I32, (TK, TQ), 0)
    q = lax.broadcasted_iota(I32, (TK, TQ), 1)
    d = off + q - k
    bucket = _t5_bucket(d)
    for h in range(hb):
        o_ref[0, :, h * TQ:(h + 1) * TQ] = jnp.where(d >= 0, _bias_rows(bucket, rb_ref, h), NEG)


def _bias_tables_prompt(rb):
    hb = rb.shape[1]
    return pl.pallas_call(
        functools.partial(_tbl_prompt_kernel, hb=hb), grid=(6,),
        in_specs=[pl.BlockSpec(memory_space=pltpu.SMEM)],
        out_specs=pl.BlockSpec((1, TK, hb * TQ), lambda t: (t, 0, 0)),
        out_shape=jax.ShapeDtypeStruct((6, TK, hb * TQ), F32),
        compiler_params=_cp(("arbitrary",)), name="bias_tables_prompt")(rb)


def _tbl_sample_kernel(rb_ref, o_ref, *, past):
    k = lax.broadcasted_iota(I32, (1, o_ref.shape[1]), 1)
    bucket = _t5_bucket(past - k)
    for h in range(N_HEADS):
        row = _bias_rows(bucket, rb_ref, h)
        o_ref[h:h + 1, :] = row
        o_ref[N_HEADS + h:N_HEADS + h + 1, :] = row


def _bias_table_sample(rb, past):
    w = past + LANES
    return pl.pallas_call(
        functools.partial(_tbl_sample_kernel, past=past),
        in_specs=[pl.BlockSpec(memory_space=pltpu.SMEM)],
        out_specs=pl.BlockSpec(memory_space=pltpu.VMEM),
        out_shape=jax.ShapeDtypeStruct((SROWS, w), F32),
        compiler_params=_cp(None), name="bias_table_sample")(rb)


def _topk_additive(score, key_ref, k):
    r, n = score.shape
    score = jnp.where(score == 0.0, 0.0, score)
    bits = pltpu.bitcast(score, I32)
    key_ref[...] = jnp.where(bits < 0, bits ^ 0x7FFFFFFF, bits)
    kf = float(k)

    def count(cond):
        return jnp.sum(jnp.where(cond, 1.0, 0.0), axis=-1, keepdims=True)

    t0 = jnp.where(count(key_ref[...] >= 0) >= kf, 0, INT_MIN).astype(I32)

    def value_bit(it, t):
        cand = t + lax.shift_left(jnp.int32(1), 30 - it)
        return jnp.where(count(key_ref[...] >= cand) >= kf, cand, t)

    t = lax.fori_loop(0, 31, value_bit, t0)
    need = kf - count(key_ref[...] > t)
    idx_bits = max(1, (n - 1).bit_length())

    def index_bit(it, x):
        cand = x + lax.shift_left(jnp.int32(1), idx_bits - 1 - it)
        idx = lax.broadcasted_iota(I32, (r, n), 1)
        below = count(jnp.logical_and(key_ref[...] == t, idx < cand))
        return jnp.where(below < need, cand, x)

    x = lax.fori_loop(0, idx_bits, index_bit, jnp.zeros((r, 1), I32))
    key = key_ref[...]
    idx = lax.broadcasted_iota(I32, (r, n), 1)
    sel = jnp.logical_or(key > t, jnp.logical_and(key == t, idx <= x))
    return jnp.where(sel, 0.0, NEG)


def _idx_prompt_kernel(iq_ref, kx_ref, wx_ref, o_ref, key_ref, *, topk):
    i = pl.program_id(1)
    lk = kx_ref.shape[1]
    lane = lax.broadcasted_iota(I32, (1, LANES), 1)
    kx = jnp.where(lane < IDX_DH, kx_ref[0], 0.0)
    kk = (kx + pltpu.roll(kx, IDX_DH, 1)).astype(BF16)
    score = jnp.zeros((TQ, lk), F32)
    for h in range(IDX_HEADS):
        slab = iq_ref[0, :, (h // 2) * LANES:(h // 2 + 1) * LANES]
        keep = (lane < IDX_DH) if h % 2 == 0 else (lane >= IDX_DH)
        qh = jnp.where(keep, slab, 0.0).astype(BF16)
        sc = lax.dot_general(qh, kk, (((1,), (1,)), ((), ())), preferred_element_type=F32)
        sc = jnp.maximum(sc * IDX_DH ** -0.5, 0.0)
        wh = wx_ref[0, :, IDX_DH + h:IDX_DH + h + 1] * IDX_HEADS ** -0.5
        score = score + sc * wh
    qpos = i * TQ + lax.broadcasted_iota(I32, (TQ, lk), 0)
    kpos = lax.broadcasted_iota(I32, (TQ, lk), 1)
    score = jnp.where(kpos <= qpos, score, NEG)
    o_ref[0] = _topk_additive(score, key_ref, topk).astype(BF16)


def _idx_prompt(h3, topk):
    b, l, _ = h3.shape
    return pl.pallas_call(
        functools.partial(_idx_prompt_kernel, topk=topk), grid=(b, l // TQ),
        in_specs=[pl.BlockSpec((1, TQ, 512), lambda bi, i: (bi, i, 4)),
                  pl.BlockSpec((1, l, LANES), lambda bi, i: (bi, 0, 24)),
                  pl.BlockSpec((1, TQ, LANES), lambda bi, i: (bi, i, 24))],
        out_specs=pl.BlockSpec((1, TQ, l), lambda bi, i: (bi, i, 0)),
        out_shape=jax.ShapeDtypeStruct((b, l, l), BF16),
        scratch_shapes=[pltpu.VMEM((TQ, l), I32)],
        compiler_params=_cp(("parallel", "arbitrary")), name="dsa_index_prompt")(h3, h3, h3)


def _flash_t_kernel(*refs, variant, scale, n_sel, lam_init):
    if variant == "A":
        qt_ref, k_ref, vt_ref, tbl_ref, lam_ref, g_ref, o_ref, qt_scr, s_scr, p_scr, m_scr, l_scr, a_scr, acc_scr = refs
    elif variant == "B":
        qt_ref, k_ref, vt_ref, tbl_ref, mask_ref, o_ref, qt_scr, s_scr, p_scr, m_scr, l_scr, a_scr, acc_scr = refs
    elif variant == "C":
        (qt_ref, k_ref, vt_ref, tbl_ref, o_ref, qt_scr, s_scr, p_scr, m_scr, l_scr, a_scr, acc_scr,
         sel_scr, mean_scr) = refs
    else:
        qt_ref, k_ref, vt_ref, tbl_ref, o_ref, qt_scr, s_scr, p_scr, m_scr, l_scr, a_scr, acc_scr = refs
    i = pl.program_id(1)
    dk, cols_total = qt_scr.shape
    groups = cols_total // TQ
    hb = tbl_ref.shape[2] // TQ
    own = (i * TQ) // TK
    parity = ((i * TQ) % TK) // TQ

    if variant == "A":
        feat = lax.broadcasted_iota(I32, (LANES, TQ), 0)
        for c in range(2):
            for h in range(N_HEADS):
                slab = qt_ref[0, h * LANES:(h + 1) * LANES, :]
                keep = (feat < DIFF_DH) if c == 0 else (feat >= DIFF_DH)
                g = c * N_HEADS + h
                qt_scr[:, g * TQ:(g + 1) * TQ] = jnp.where(keep, slab, 0.0).astype(BF16)
    else:
        for h in range(N_HEADS):
            qt_scr[:, h * TQ:(h + 1) * TQ] = qt_ref[0, h * dk:(h + 1) * dk, :].astype(BF16)
    m_scr[...] = jnp.full(m_scr.shape, NEG, F32)
    l_scr[...] = jnp.zeros(l_scr.shape, F32)
    acc_scr[...] = jnp.zeros(acc_scr.shape, F32)

    if variant == "C":
        nblk = k_ref.shape[1] // TK
        nb_pad = mean_scr.shape[0]
        mean_scr[...] = jnp.zeros(mean_scr.shape, F32)
        for n in range(nblk):
            mean_scr[n:n + 1, :] = jnp.sum(k_ref[0, n * TK:(n + 1) * TK, :], axis=0, keepdims=True) * (1.0 / TK)
        nrow = lax.broadcasted_iota(I32, (nb_pad, TQ), 0)
        for h in range(N_HEADS):
            gate = jnp.dot(mean_scr[...], qt_ref[0, h * LANES:(h + 1) * LANES, :],
                           preferred_element_type=F32)
            rank = jnp.zeros((nb_pad, TQ), F32)
            for n2 in range(nblk):
                other = gate[n2:n2 + 1, :]
                beats = jnp.where(other > gate, 1.0,
                                  jnp.where(other == gate, jnp.where(n2 < nrow, 1.0, 0.0), 0.0))
                rank = rank + beats * (n2 < own).astype(F32)
            add = jnp.where(nrow < own, jnp.where(rank < n_sel, 0.0, NEG), 0.0)
            for n in range(nblk):
                sel_scr[n, :, h * TQ:(h + 1) * TQ] = add[n:n + 1, :]

    def step(j, carry):
        off = pl.multiple_of(j * TK, TK)
        kb = k_ref[0, pl.ds(off, TK), :].astype(BF16)
        s_scr[...] = jnp.dot(kb, qt_scr[...], preferred_element_type=F32)
        t = jnp.minimum(own - j, 2) * 2 + parity
        for g in range(groups):
            cols = slice(g * TQ, (g + 1) * TQ)
            hcol = (g % hb) * TQ
            s = s_scr[:, cols] * scale + tbl_ref[t, :, hcol:hcol + TQ]
            if variant == "B":
                s = s + mask_ref[0, pl.ds(off, TK), :].astype(F32)
            if variant == "C":
                s = s + sel_scr[j, :, cols]
            m_prev = m_scr[:, cols]
            m_new = jnp.maximum(m_prev, jnp.max(s, axis=0, keepdims=True))
            alpha = jnp.exp(m_prev - m_new)
            p = jnp.exp(s - m_new)
            l_scr[:, cols] = alpha * l_scr[:, cols] + jnp.sum(p, axis=0, keepdims=True)
            m_scr[:, cols] = m_new
            a_scr[:, cols] = alpha
            p_scr[:, cols] = p.astype(BF16)
        vtb = vt_ref[0, :, pl.ds(off, TK)].astype(BF16)
        acc_scr[...] = acc_scr[...] * a_scr[...] + jnp.dot(vtb, p_scr[...], preferred_element_type=F32)
        return carry

    lax.fori_loop(0, own + 1, step, 0)

    def head_out(g):
        cols = slice(g * TQ, (g + 1) * TQ)
        return jnp.transpose(acc_scr[:, cols] / l_scr[:, cols])

    dv = acc_scr.shape[0]
    lane = lax.broadcasted_iota(I32, (TQ, LANES), 1)
    if variant == "A":
        lv = lam_ref[...]
        lam = (jnp.exp(jnp.sum(lv[0:1] * lv[1:2], axis=-1, keepdims=True))
               - jnp.exp(jnp.sum(lv[2:3] * lv[3:4], axis=-1, keepdims=True)) + lam_init)
        for h in range(N_HEADS):
            o = head_out(h) - lam * head_out(N_HEADS + h)
            o = o * lax.rsqrt(jnp.mean(o * o, axis=-1, keepdims=True) + LN_EPS)
            o_ref[0, :, h * LANES:(h + 1) * LANES] = (o * g_ref[...] * (1.0 - lam_init)).astype(o_ref.dtype)
    elif variant == "C":
        for pair in range(N_HEADS // 2):
            a0 = pltpu.roll(head_out(2 * pair), MOBA_DH, 1)
            a1 = head_out(2 * pair + 1)
            o_ref[0, :, pair * LANES:(pair + 1) * LANES] = jnp.where(lane < MOBA_DH, a0, a1).astype(o_ref.dtype)
    else:
        for h in range(N_HEADS):
            o_ref[0, :, h * dv:(h + 1) * dv] = head_out(h).astype(o_ref.dtype)


def _flash_t(variant, qt, k3, vt, tbl, *, k_blk, k_width, out_width, groups, scale,
             mask_t=None, lam=None, subln_g=None, lam_init=0.0, n_sel=0, name):
    b, wq, l = qt.shape
    dv = vt.shape[1]
    cols = groups * TQ
    in_specs = [pl.BlockSpec((1, wq, TQ), lambda bi, i: (bi, 0, i)),
                pl.BlockSpec((1, l, k_width), lambda bi, i: (bi, 0, k_blk)),
                pl.BlockSpec((1, dv, l), lambda bi, i: (bi, 0, 0)),
                pl.BlockSpec(tbl.shape, lambda bi, i: (0, 0, 0))]
    args = [qt, k3, vt, tbl]
    if variant == "A":
        in_specs += [pl.BlockSpec(lam.shape, lambda bi, i: (0, 0)), pl.BlockSpec((1, LANES), lambda bi, i: (0, 0))]
        args += [lam, subln_g.reshape(1, LANES)]
    if variant == "B":
        in_specs.append(pl.BlockSpec((1, l, TQ), lambda bi, i: (bi, 0, i)))
        args.append(mask_t)
    scratch = [pltpu.VMEM((k_width, cols), BF16), pltpu.VMEM((TK, cols), F32), pltpu.VMEM((TK, cols), BF16),
               pltpu.VMEM((1, cols), F32), pltpu.VMEM((1, cols), F32), pltpu.VMEM((1, cols), F32),
               pltpu.VMEM((dv, cols), F32)]
    if variant == "C":
        nb_pad = max(8, l // TK)
        scratch += [pltpu.VMEM((nb_pad, 1, cols), F32), pltpu.VMEM((nb_pad, LANES), F32)]
    kern = functools.partial(_flash_t_kernel, variant=variant, scale=scale, n_sel=n_sel, lam_init=lam_init)
    return pl.pallas_call(
        kern, grid=(b, l // TQ), in_specs=in_specs,
        out_specs=pl.BlockSpec((1, TQ, out_width), lambda bi, i: (bi, i, 0)),
        out_shape=jax.ShapeDtypeStruct((b, l, out_width), BF16),
        scratch_shapes=scratch, compiler_params=_cp(("parallel", "arbitrary")), name=name)(*args)


def _page_copy(pool_ref, pt_ref, buf, sem, layer, seq, slot, row, page):
    return pltpu.make_async_copy(pool_ref.at[pt_ref[seq, page], layer], buf.at[slot, row], sem.at[slot])


def _fetch_pages(pool_ref, pt_ref, buf, sem, layer, n_pages, seqs=1):
    s = pl.program_id(0)
    slot = s % 2

    def start(step, into):
        def one_seq(b, c):
            for p in range(n_pages):
                _page_copy(pool_ref, pt_ref, buf, sem, layer, step * seqs + b, into, b * n_pages + p, p).start()
            return c
        lax.fori_loop(0, seqs, one_seq, 0)

    @pl.when(s == 0)
    def _():
        start(0, 0)

    @pl.when(s + 1 < pl.num_programs(0))
    def _():
        start(s + 1, 1 - slot)

    def wait_seq(b, c):
        for p in range(n_pages):
            _page_copy(pool_ref, pt_ref, buf, sem, layer, 0, slot, b * n_pages + p, p).wait()
        return c
    lax.fori_loop(0, seqs, wait_seq, 0)
    return slot


def _chunk_t(buf, slot, c):
    pages = [buf[slot, c * CHUNK_PAGES + p] for p in range(CHUNK_PAGES)]
    return jnp.concatenate(pages, axis=1).astype(BF16)


def _sample_attn_kernel(*refs, variant, layer, pieces, v_lo, v_hi, scale, n_sel, transposed):
    if variant == "B":
        pt_ref, q_ref, pool_ref, new_ref, tb_ref, ex_ref, o_ref, buf, sem, s_scr, p_scr = refs
    elif variant == "C":
        pt_ref, q_ref, pool_ref, new_ref, tb_ref, o_ref, buf, sem, s_scr, p_scr, mean_scr, e_scr = refs
    else:
        pt_ref, q_ref, pool_ref, new_ref, tb_ref, o_ref, buf, sem, s_scr, p_scr = refs
    n_pages = buf.shape[1]
    past = n_pages * PAGE
    ckeys = CHUNK_PAGES * PAGE
    n_chunks = n_pages // CHUNK_PAGES
    slot = _fetch_pages(pool_ref, pt_ref, buf, sem, layer, n_pages)

    qf = q_ref[0]
    qb = qf.astype(BF16)

    if variant == "C":
        nblk = past // MOBA_BLOCK

        @pl.when(pl.program_id(0) == 0)
        def _():
            blk = lax.broadcasted_iota(I32, e_scr.shape, 0)
            key = lax.broadcasted_iota(I32, e_scr.shape, 1)
            e_scr[...] = jnp.where(key // MOBA_BLOCK == blk, 1.0, 0.0).astype(BF16)

    for c in range(n_chunks):
        s = None
        if transposed:
            kt = _chunk_t(buf, slot, c)
            for lo, hi in pieces:
                part = jnp.dot(qb[:, lo:hi], kt[lo:hi, :], preferred_element_type=F32)
                s = part if s is None else s + part
        else:
            kc = buf[slot, c * CHUNK_PAGES:(c + 1) * CHUNK_PAGES].reshape(ckeys, buf.shape[3])
            for lo, hi in pieces:
                part = lax.dot_general(qb[:, lo:hi], kc[:, lo:hi].astype(BF16), (((1,), (1,)), ((), ())),
                                       preferred_element_type=F32)
                s = part if s is None else s + part
        s_scr[:, c * ckeys:(c + 1) * ckeys] = s
        if variant == "C":
            per = ckeys // MOBA_BLOCK
            for n in range(per):
                mean_scr[c * per + n:c * per + n + 1, :] = (
                    jnp.sum(kc[n * MOBA_BLOCK:(n + 1) * MOBA_BLOCK, :], axis=0, keepdims=True) * (1.0 / MOBA_BLOCK))

    s = s_scr[...] * scale
    new = new_ref[0]
    s_new = None
    for lo, hi in pieces:
        part = jnp.sum(qf[:, lo:hi] * new[:, lo:hi], axis=-1, keepdims=True)
        s_new = part if s_new is None else s_new + part
    s_new = s_new * scale
    if tb_ref is not None:
        s = s + tb_ref[:, 0:past]
        s_new = s_new + tb_ref[:, past:past + 1]
    if variant == "B":
        s = s + ex_ref[0, :, 0:past]
        s_new = s_new + ex_ref[0, :, past:past + 1]
    if variant == "C":
        gate = lax.dot_general(qf, mean_scr[...], (((1,), (1,)), ((), ())), preferred_element_type=F32)
        nlane = lax.broadcasted_iota(I32, gate.shape, 1)
        rank = jnp.zeros(gate.shape, F32)
        for n2 in range(nblk):
            col = gate[:, n2:n2 + 1]
            rank = rank + jnp.where(col > gate, 1.0,
                                    jnp.where(col == gate, jnp.where(n2 < nlane, 1.0, 0.0), 0.0))
        chosen = jnp.where(rank < n_sel, 1.0, 0.0).astype(BF16)
        keep = jnp.dot(chosen, e_scr[...], preferred_element_type=F32)
        s = s + (keep - 1.0) * (-NEG)

    m = jnp.maximum(jnp.max(s, axis=-1, keepdims=True), s_new)
    p = jnp.exp(s - m)
    p_new = jnp.exp(s_new - m)
    denom = jnp.sum(p, axis=-1, keepdims=True) + p_new
    p_scr[...] = p.astype(BF16)
    acc = p_new * new[:, v_lo:v_hi]
    for c in range(n_chunks):
        pc = p_scr[:, c * ckeys:(c + 1) * ckeys]
        if transposed:
            vt = _chunk_t(buf, slot, c)[v_lo:v_hi, :]
            acc = acc + lax.dot_general(pc, vt, (((1,), (1,)), ((), ())), preferred_element_type=F32)
        else:
            vc = buf[slot, c * CHUNK_PAGES:(c + 1) * CHUNK_PAGES].reshape(ckeys, buf.shape[3])[:, v_lo:v_hi]
            acc = acc + jnp.dot(pc, vc.astype(BF16), preferred_element_type=F32)
    o_ref[0] = acc / denom


def _sample_attn(variant, q, pool, layer, page_table, new_rows, tb, *, pieces, v_lo, v_hi, scale,
                 extra=None, n_sel=0, transposed=False, name):
    ns, _, dq = q.shape
    n_pages = page_table.shape[1]
    past = n_pages * PAGE
    w = new_rows.shape[2]
    page_shape = (w, PAGE) if transposed else (PAGE, w)
    in_specs = [pl.BlockSpec((1, SROWS, dq), lambda s, pt: (s, 0, 0)),
                pl.BlockSpec(memory_space=pl.ANY),
                pl.BlockSpec((1, 1, w), lambda s, pt: (s, 0, 0))]
    args = [q, pool, new_rows]
    if tb is not None:
        in_specs.append(pl.BlockSpec(tb.shape, lambda s, pt: (0, 0)))
        args.append(tb)
    if variant == "B":
        in_specs.append(pl.BlockSpec((1, 1, past + LANES), lambda s, pt: (s, 0, 0)))
        args.append(extra)
    scratch = [pltpu.VMEM((2, n_pages) + page_shape, F32), pltpu.SemaphoreType.DMA((2,)),
               pltpu.VMEM((SROWS, past), F32), pltpu.VMEM((SROWS, past), BF16)]
    if variant == "C":
        nblk = past // MOBA_BLOCK
        scratch += [pltpu.VMEM((nblk, w), F32), pltpu.VMEM((nblk, past), BF16)]
    kern = functools.partial(_sample_attn_kernel, variant=variant, layer=layer, pieces=pieces,
                             v_lo=v_lo, v_hi=v_hi, scale=scale, n_sel=n_sel, transposed=transposed)
    if tb is None:
        inner = kern

        def kern(pt_ref, q_ref, pool_ref, new_ref, *rest):
            return inner(pt_ref, q_ref, pool_ref, new_ref, None, *rest)

    return pl.pallas_call(
        kern,
        grid_spec=pltpu.PrefetchScalarGridSpec(
            num_scalar_prefetch=1, grid=(ns,), in_specs=in_specs,
            out_specs=pl.BlockSpec((1, SROWS, v_hi - v_lo), lambda s, pt: (s, 0, 0)),
            scratch_shapes=scratch),
        out_shape=jax.ShapeDtypeStruct((ns, SROWS, v_hi - v_lo), F32),
        compiler_params=_cp(("arbitrary",)), name=name)(page_table, *args)


IDX_SEQS = 8


def _idx_sample_kernel(pt_ref, iq_ref, w_ref, pool_ref, new_ref, o_ref, buf, sem, row_scr, key_scr, *,
                       layer, topk, n_pages):
    past = n_pages * PAGE
    ckeys = CHUNK_PAGES * PAGE
    slot = _fetch_pages(pool_ref, pt_ref, buf, sem, layer, n_pages, seqs=IDX_SEQS)
    lane = lax.broadcasted_iota(I32, (1, LANES), 1)
    for b in range(IDX_SEQS):
        iqf = iq_ref[b]
        iqb = iqf.astype(BF16)
        wh = w_ref[b][:, 0:1] * IDX_HEADS ** -0.5
        for c in range(n_pages // CHUNK_PAGES):
            pages = [buf[slot, b * n_pages + c * CHUNK_PAGES + p] for p in range(CHUNK_PAGES)]
            kt = jnp.concatenate(pages, axis=1).astype(BF16)
            sc = jnp.maximum(jnp.dot(iqb, kt, preferred_element_type=F32) * IDX_DH ** -0.5, 0.0)
            row_scr[b:b + 1, c * ckeys:(c + 1) * ckeys] = jnp.sum(sc * wh, axis=0, keepdims=True)
        sc_new = jnp.maximum(jnp.sum(iqf * new_ref[b], axis=-1, keepdims=True) * IDX_DH ** -0.5, 0.0)
        score_new = jnp.sum(sc_new * wh, axis=0, keepdims=True)
        row_scr[b:b + 1, past:past + LANES] = jnp.where(lane == 0, score_new, -3e38)
    o_ref[0] = _topk_additive(row_scr[...], key_scr, topk)


def _idx_sample(iq, iw, pool_t, layer, page_table, new_rows, topk):
    ns = iq.shape[0]
    n_pages = page_table.shape[1]
    past = n_pages * PAGE
    wide = past + LANES
    out = pl.pallas_call(
        functools.partial(_idx_sample_kernel, layer=layer, topk=topk, n_pages=n_pages),
        grid_spec=pltpu.PrefetchScalarGridSpec(
            num_scalar_prefetch=1, grid=(ns // IDX_SEQS,),
            in_specs=[pl.BlockSpec((IDX_SEQS, SROWS, IDX_DH), lambda s, pt: (s, 0, 0)),
                      pl.BlockSpec((IDX_SEQS, SROWS, LANES), lambda s, pt: (s, 0, 0)),
                      pl.BlockSpec(memory_space=pl.ANY),
                      pl.BlockSpec((IDX_SEQS, 1, IDX_DH), lambda s, pt: (s, 0, 0))],
            out_specs=pl.BlockSpec((1, IDX_SEQS, wide), lambda s, pt: (s, 0, 0)),
            scratch_shapes=[pltpu.VMEM((2, IDX_SEQS * n_pages, IDX_DH, PAGE), F32), pltpu.SemaphoreType.DMA((2,)),
                            pltpu.VMEM((IDX_SEQS, wide), F32), pltpu.VMEM((IDX_SEQS, wide), I32)]),
        out_shape=jax.ShapeDtypeStruct((ns // IDX_SEQS, IDX_SEQS, wide), F32),
        compiler_params=_cp(("arbitrary",)), name="dsa_index_sample")(page_table, iq, iw, pool_t, new_rows)
    return out.reshape(ns, 1, wide)


def _diff_post_kernel(o_ref, lam_ref, g_ref, out_ref, *, lam_init):
    lv = lam_ref[...]
    lam = (jnp.exp(jnp.sum(lv[0:1] * lv[1:2], axis=-1, keepdims=True))
           - jnp.exp(jnp.sum(lv[2:3] * lv[3:4], axis=-1, keepdims=True)) + lam_init)
    o = o_ref[...]
    d = o[:, 0:N_HEADS, :] - lam * o[:, N_HEADS:2 * N_HEADS, :]
    d = d * lax.rsqrt(jnp.mean(d * d, axis=-1, keepdims=True) + LN_EPS)
    out_ref[...] = d * g_ref[...] * (1.0 - lam_init)


def _diff_post(o, lam, g, lam_init):
    ns = o.shape[0]
    vm = pl.BlockSpec(memory_space=pltpu.VMEM)
    return pl.pallas_call(
        functools.partial(_diff_post_kernel, lam_init=lam_init), in_specs=[vm, vm, vm], out_specs=vm,
        out_shape=jax.ShapeDtypeStruct((ns, N_HEADS, LANES), F32),
        compiler_params=_cp(None), name="diff_post_sample")(o, lam, g.reshape(1, LANES))


def _rope_group(x, cos, sin):
    lane = lax.broadcasted_iota(I32, x.shape, 1)
    half = MLA_ROPE // 2
    other = jnp.where(lane < half, pltpu.roll(x, LANES - half, 1), pltpu.roll(x, half, 1))
    return x * cos + other * sin


def _mla_prep_kernel(h_ref, qg_ref, kg_ref, cos_ref, sin_ref, qn_ref, kv_ref):
    qa = h_ref[:, 0:MLA_Q_RANK]
    qn = qa * lax.rsqrt(jnp.mean(qa * qa, axis=-1, keepdims=True) + LN_EPS) * qg_ref[...]
    qn_ref[...] = qn.astype(BF16)
    kva = h_ref[:, MLA_Q_RANK:MLA_Q_RANK + MLA_KV_RANK]
    kv_ref[:, 0:MLA_KV_RANK] = kva * lax.rsqrt(jnp.mean(kva * kva, axis=-1, keepdims=True) + LN_EPS) * kg_ref[...]
    kr = h_ref[:, MLA_Q_RANK + MLA_KV_RANK + LANES:MLA_Q_RANK + MLA_KV_RANK + 2 * LANES]
    kv_ref[:, MLA_KV_RANK:MLA_KV_RANK + LANES] = _rope_group(kr, cos_ref[...], sin_ref[...])


def _mla_prep(h, qg, kg, cos, sin):
    m = h.shape[0]
    tm = _tile(m, 640, 16)
    wide = MLA_Q_RANK + MLA_KV_RANK + 2 * LANES
    return pl.pallas_call(
        _mla_prep_kernel, grid=(m // tm,),
        in_specs=[pl.BlockSpec((tm, wide), lambda i: (i, 1)),
                  pl.BlockSpec((1, MLA_Q_RANK), lambda i: (0, 0)), pl.BlockSpec((1, MLA_KV_RANK), lambda i: (0, 0)),
                  pl.BlockSpec((tm, LANES), lambda i: (i, 0)), pl.BlockSpec((tm, LANES), lambda i: (i, 0))],
        out_specs=[pl.BlockSpec((tm, MLA_Q_RANK), lambda i: (i, 0)),
                   pl.BlockSpec((tm, MLA_KV_RANK + LANES), lambda i: (i, 0))],
        out_shape=[jax.ShapeDtypeStruct((m, MLA_Q_RANK), BF16),
                   jax.ShapeDtypeStruct((m, MLA_KV_RANK + LANES), F32)],
        compiler_params=_cp(("parallel",)), name="mla_prep")(
            h, qg.reshape(1, -1), kg.reshape(1, -1), cos, sin)


def _mla_q_kernel(qf_ref, wuk_ref, cos_ref, sin_ref, o_ref):
    width = MLA_KV_RANK + LANES
    for h in range(N_HEADS):
        nope = qf_ref[:, h * LANES:(h + 1) * LANES].astype(BF16)
        o_ref[:, h * width:h * width + MLA_KV_RANK] = jnp.dot(
            nope, wuk_ref[h], preferred_element_type=F32).astype(o_ref.dtype)
        rope = qf_ref[:, (N_HEADS + h) * LANES:(N_HEADS + h + 1) * LANES]
        o_ref[:, h * width + MLA_KV_RANK:(h + 1) * width] = _rope_group(
            rope, cos_ref[...], sin_ref[...]).astype(o_ref.dtype)


def _mla_q(qf, wuk_t, cos, sin, out_dtype):
    m = qf.shape[0]
    tm = _tile(m, 640, 16)
    width = N_HEADS * (MLA_KV_RANK + LANES)
    return pl.pallas_call(
        _mla_q_kernel, grid=(m // tm,),
        in_specs=[pl.BlockSpec((tm, qf.shape[1]), lambda i: (i, 0)),
                  pl.BlockSpec(wuk_t.shape, lambda i: (0, 0, 0)),
                  pl.BlockSpec((tm, LANES), lambda i: (i, 0)), pl.BlockSpec((tm, LANES), lambda i: (i, 0))],
        out_specs=pl.BlockSpec((tm, width), lambda i: (i, 0)),
        out_shape=jax.ShapeDtypeStruct((m, width), out_dtype),
        compiler_params=_cp(("parallel",)), name="mla_q")(qf, wuk_t, cos, sin)


def _mla_uv_kernel(o_ref, w_ref, d_ref):
    for h in range(N_HEADS):
        d_ref[:, h * MLA_VDH:(h + 1) * MLA_VDH] = jnp.dot(
            o_ref[:, h * MLA_KV_RANK:(h + 1) * MLA_KV_RANK], w_ref[h], preferred_element_type=F32).astype(BF16)


def _mla_uv(o_lat, wuv):
    m = o_lat.shape[0]
    tm = _tile(m, 640, 16)
    return pl.pallas_call(
        _mla_uv_kernel, grid=(m // tm,),
        in_specs=[pl.BlockSpec((tm, o_lat.shape[1]), lambda i: (i, 0)), pl.BlockSpec(wuv.shape, lambda i: (0, 0, 0))],
        out_specs=pl.BlockSpec((tm, N_HEADS * MLA_VDH), lambda i: (i, 0)),
        out_shape=jax.ShapeDtypeStruct((m, N_HEADS * MLA_VDH), BF16),
        compiler_params=_cp(("parallel",)), name="mla_uv")(o_lat, wuv)


def _prep_w_in_even(w):
    d = w.shape[0]
    pad = jnp.zeros((d, 3200 - 3144), w.dtype)
    cols = [w[:, 0:1024], w[:, 1280:2304], w[:, 2560:3072], w[:, 1024:1280], w[:, 2304:2560], w[:, 3072:3144], pad]
    return jnp.concatenate(cols, axis=1).astype(BF16)


def _prep_w_in_odd(w):
    d = w.shape[0]
    mq = jnp.pad(w[:, 0:512].reshape(d, N_HEADS, MOBA_DH), ((0, 0), (0, 0), (0, LANES - MOBA_DH))).reshape(d, -1)
    kr = jnp.pad(w[:, 1408:1472], ((0, 0), (0, LANES - MLA_ROPE)))
    return jnp.concatenate([mq, w[:, 640:1152], w[:, 1152:1408], w[:, 512:640], kr], axis=1).astype(BF16)


def _prep_w_q_up(w):
    r = w.shape[0]
    w3 = w.reshape(r, N_HEADS, MLA_NOPE + MLA_ROPE)
    nope = w3[:, :, :MLA_NOPE].reshape(r, -1)
    rope = jnp.pad(w3[:, :, MLA_NOPE:], ((0, 0), (0, 0), (0, LANES - MLA_ROPE))).reshape(r, -1)
    return jnp.concatenate([nope, rope], axis=1).astype(BF16)


def _rope_tables(pos):
    half = MLA_ROPE // 2
    freq = jnp.power(ROPE_THETA, -jnp.arange(half, dtype=F32) / half)
    ang = pos.astype(F32)[:, None] * freq
    cos, sin = jnp.cos(ang), jnp.sin(ang)
    zero = jnp.zeros((pos.shape[0], LANES - MLA_ROPE), F32)
    return jnp.concatenate([cos, cos, zero], axis=1), jnp.concatenate([-sin, sin, zero], axis=1)


def _sample_rows(x, n_rows=SROWS):
    return jnp.pad(x, ((0, 0), (0, n_rows - x.shape[1]), (0, 0)))


def kernel(x_prompt, x_sample, cache_diff_kv, cache_dsa_kv, cache_dsa_idx, cache_moba_kv, cache_mla, page_table, p_prompt, p_sample, rel_bias, w_in_even, diff_lambda, diff_subln_g, w_out_even, ffn_w_in, ffn_w_out, w_in_odd, mla_q_norm_g, mla_kv_norm_g, mla_w_q_up, mla_w_uk, mla_w_uv, w_out_odd, moe_router, moe_w_in, moe_w_out, ln1_g, ln1_b, ln2_g, ln2_b, ple_w_gate, ple_w_proj):
    b, l, d = x_prompt.shape
    ns, dec_seq, _ = x_sample.shape
    depth = ln1_g.shape[0]
    n_pages = page_table.shape[1]
    past = n_pages * PAGE
    n_prompt = b * l
    assert dec_seq == 1 and past % MOBA_BLOCK == 0 and n_pages % CHUNK_PAGES == 0
    assert l % TK == 0 and TK == MOBA_BLOCK and TK == 2 * TQ
    alpha = (2 * depth) ** 0.25

    x = jnp.concatenate([x_prompt.reshape(n_prompt, d), x_sample.reshape(ns, d)], axis=0)
    xb = x.astype(BF16)
    ple = jnp.concatenate([p_prompt.reshape(depth, n_prompt, -1), p_sample.reshape(depth, ns, -1)], axis=1).astype(BF16)

    tbl8 = _bias_tables_prompt(rel_bias)
    tbl0 = _bias_tables_prompt(jnp.zeros((REL_BUCKETS, 1), F32))
    tbs = _bias_table_sample(rel_bias, past)
    pos = jnp.concatenate([jnp.tile(jnp.arange(l), b), jnp.full((ns,), past)])
    cos, sin = _rope_tables(pos)
    lane = jnp.arange(LANES)
    idx_pool_t = jnp.swapaxes(cache_dsa_idx, 2, 3)
    mla_pool_t = jnp.swapaxes(cache_mla, 2, 3)
    n_tiles = (2 * (n_prompt + ns) + N_EXPERTS * (MOE_TILE - 1)) // MOE_TILE
    moe_w_out_b = moe_w_out.astype(BF16)

    rows_diff, rows_dsa, rows_idx, rows_moba, rows_mla = [], [], [], [], []
    for i in range(depth):
        j = i // 2
        if i % 2 == 0:
            lam_init = 0.8 - 0.6 * math.exp(-0.3 * i)
            h = _matmul(xb, _prep_w_in_even(w_in_even[j]), F32, "in_proj_even", tn_pref=640)
            r_diff, r_dsa, r_idx = h[:, 2560:2816], h[:, 2816:3072], h[:, 3072:3072 + IDX_DH]
            rows_diff.append(r_diff); rows_dsa.append(r_dsa); rows_idx.append(r_idx)
            hp = h[:n_prompt].reshape(b, l, -1)
            hs = h[n_prompt:]
            a_p = _flash_t("A", jnp.swapaxes(hp[:, :, 0:1024], 1, 2), hp, jnp.swapaxes(hp[:, :, 2688:2816], 1, 2),
                           tbl8, k_blk=20, k_width=LANES, out_width=1024, groups=2 * N_HEADS,
                           scale=DIFF_DH ** -0.5, lam=diff_lambda[j], subln_g=diff_subln_g[j], lam_init=lam_init,
                           name="diff_attn_prompt")
            topk = min(DSA_TOPK, l // 4)
            sel_t = jnp.swapaxes(_idx_prompt(hp, topk), 1, 2)
            s_p = _flash_t("B", jnp.swapaxes(hp[:, :, 1024:2048], 1, 2), hp, jnp.swapaxes(hp[:, :, 2944:3072], 1, 2),
                           tbl8, k_blk=22, k_width=LANES, out_width=1024, groups=N_HEADS, scale=DSA_DH ** -0.5,
                           mask_t=sel_t, name="dsa_attn_prompt")
            dq = hs[:, 0:1024].reshape(ns, N_HEADS, LANES)
            q_a = jnp.concatenate([jnp.where(lane < DIFF_DH, dq, 0.0), jnp.where(lane >= DIFF_DH, dq, 0.0)], axis=1)
            o_a = _sample_attn("A", q_a, cache_diff_kv, j, page_table, hs[:, None, 2560:2816], tbs,
                               pieces=((0, LANES),), v_lo=LANES, v_hi=2 * LANES, scale=DIFF_DH ** -0.5,
                               name="diff_attn_sample")
            a_s = _diff_post(o_a, diff_lambda[j], diff_subln_g[j], lam_init).reshape(ns, -1)
            iq = _sample_rows(hs[:, 2048:2560].reshape(ns, IDX_HEADS, IDX_DH))
            iw = _sample_rows(jnp.broadcast_to(hs[:, 3072 + IDX_DH:3072 + IDX_DH + IDX_HEADS, None],
                                               (ns, IDX_HEADS, LANES)))
            topk_s = min(DSA_TOPK, (past + 1) // 4)
            sel_s = _idx_sample(iq, iw, idx_pool_t, j, page_table, hs[:, None, 3072:3072 + IDX_DH], topk_s)
            q_s = _sample_rows(hs[:, 1024:2048].reshape(ns, N_HEADS, LANES))
            o_s = _sample_attn("B", q_s, cache_dsa_kv, j, page_table, hs[:, None, 2816:3072], tbs,
                               pieces=((0, LANES),), v_lo=LANES, v_hi=2 * LANES, scale=DSA_DH ** -0.5,
                               extra=sel_s, name="dsa_attn_sample")
            s_s = o_s[:, :N_HEADS].reshape(ns, -1)
            mix_p = jnp.concatenate([a_p.reshape(n_prompt, -1), s_p.reshape(n_prompt, -1)], axis=1)
            mix_s = jnp.concatenate([a_s, s_s], axis=1).astype(BF16)
            mix = jnp.concatenate([mix_p, mix_s], axis=0)
            hout = _matmul(mix, w_out_even[j].astype(BF16), F32, "out_proj_even")
        else:
            h = _matmul(xb, _prep_w_in_odd(w_in_odd[j]), F32, "in_proj_odd")
            r_moba = h[:, 1792:1920]
            rows_moba.append(r_moba)
            qn, kvrow = _mla_prep(h, mla_q_norm_g[j], mla_kv_norm_g[j], cos, sin)
            rows_mla.append(kvrow[:, :MLA_KV_RANK + MLA_ROPE])
            qf = _matmul(qn, _prep_w_q_up(mla_w_q_up[j]), F32, "mla_q_up")
            wuk_t = jnp.transpose(mla_w_uk[j], (1, 2, 0)).astype(BF16)
            wuv = jnp.transpose(mla_w_uv[j], (1, 0, 2)).astype(BF16)
            qmla = _mla_q(qf, wuk_t, cos, sin, F32)
            hp = h[:n_prompt].reshape(b, l, -1)
            hs = h[n_prompt:]
            m_p = _flash_t("C", jnp.swapaxes(hp[:, :, 0:1024], 1, 2), hp, jnp.swapaxes(hp[:, :, 1792:1920], 1, 2),
                           tbl8, k_blk=14, k_width=LANES, out_width=N_HEADS * MOBA_DH, groups=N_HEADS,
                           scale=MOBA_DH ** -0.5, n_sel=min(MOBA_TOPK, l // MOBA_BLOCK), name="moba_attn_prompt")
            width = MLA_KV_RANK + LANES
            kv_p = kvrow[:n_prompt].reshape(b, l, -1)
            o_lat_p = _flash_t("D", jnp.swapaxes(qmla[:n_prompt].reshape(b, l, -1), 1, 2), kv_p,
                               jnp.swapaxes(kv_p[:, :, :MLA_KV_RANK], 1, 2), tbl0, k_blk=0, k_width=width,
                               out_width=N_HEADS * MLA_KV_RANK, groups=N_HEADS,
                               scale=(MLA_NOPE + MLA_ROPE) ** -0.5, name="mla_attn_prompt")
            q_c = _sample_rows(hs[:, 0:1024].reshape(ns, N_HEADS, LANES))
            o_c = _sample_attn("C", q_c, cache_moba_kv, j, page_table, hs[:, None, 1792:1920], tbs,
                               pieces=((0, LANES),), v_lo=0, v_hi=LANES, scale=MOBA_DH ** -0.5,
                               n_sel=min(MOBA_TOPK, (past + 1) // MOBA_BLOCK), name="moba_attn_sample")
            m_s = o_c[:, :N_HEADS, MOBA_DH:].reshape(ns, -1)
            q_d = _sample_rows(qmla[n_prompt:].reshape(ns, N_HEADS, width)[:, :, :MLA_KV_RANK + MLA_ROPE])
            o_d = _sample_attn("D", q_d, mla_pool_t, j, page_table, kvrow[n_prompt:, None, :MLA_KV_RANK + MLA_ROPE],
                               None, pieces=((0, MLA_KV_RANK), (MLA_KV_RANK, MLA_KV_RANK + MLA_ROPE)),
                               v_lo=0, v_hi=MLA_KV_RANK, scale=(MLA_NOPE + MLA_ROPE) ** -0.5, transposed=True,
                               name="mla_attn_sample")
            o_lat = jnp.concatenate([o_lat_p.reshape(n_prompt, -1),
                                     o_d[:, :N_HEADS].reshape(ns, -1).astype(BF16)], axis=0)
            dmla = _mla_uv(o_lat, wuv)
            m_all = jnp.concatenate([m_p.reshape(n_prompt, -1), m_s.astype(BF16)], axis=0)
            mix = jnp.concatenate([m_all, dmla], axis=1)
            hout = _matmul(mix, w_out_odd[j].astype(BF16), F32, "out_proj_odd")

        x, xb = _deepnorm_ln(x, hout, ln1_g[i], ln1_b[i], alpha, "deepnorm_ln1")
        if i % 2 == 0:
            mid = _swiglu(xb, ffn_w_in[j][None].astype(BF16), "ffn_in")
            f = _matmul(mid, ffn_w_out[j].astype(BF16), F32, "ffn_out")
            x, xb = _deepnorm_ln(x, f, ln2_g[i], ln2_b[i], alpha, "deepnorm_ln2")
        else:
            router = jnp.pad(moe_router[j], ((0, 0), (0, LANES - N_EXPERTS))).astype(BF16)
            route = _router(xb, router, "moe_router")
            slot1, slot2, token_of_slot, tile_expert, n_used = _dispatch_plan(route, n_tiles)
            xs = _gather_rows(x, token_of_slot)
            mid = _grouped_swiglu(xs, moe_w_in, j, tile_expert, n_used)
            ys = _grouped_matmul(mid, moe_w_out_b, j, tile_expert, n_used)
            x, xb = _moe_combine_ln(ys, slot1, slot2, route, x, ln2_g[i], ln2_b[i], alpha)
        x, xb = _ple(xb, x, ple_w_gate[i].astype(BF16), ple[i], ple_w_proj[i].astype(BF16), "ple_gate")

    def split(rows, width):
        r = jnp.stack(rows, axis=0)
        rp = jnp.transpose(r[:, :n_prompt].reshape(-1, b, l, width), (1, 0, 2, 3))
        rs = jnp.transpose(r[:, n_prompt:].reshape(-1, ns, 1, width), (1, 0, 2, 3))
        return rp, rs

    dp, ds = split(rows_diff, 256)
    sp, ss = split(rows_dsa, 256)
    ip, is_ = split(rows_idx, IDX_DH)
    mp, ms = split(rows_moba, 2 * MOBA_DH)
    lp, ls = split(rows_mla, MLA_KV_RANK + MLA_ROPE)
    return (x[:n_prompt].reshape(b, l, d), x[n_prompt:].reshape(ns, 1, d), dp, ds, sp, ss, ip, is_, mp, ms, lp, ls)
```

```python
import functools
import math

import jax
import jax.numpy as jnp
from jax import lax
from jax.experimental import pallas as pl
from jax.experimental.pallas import tpu as pltpu

F32, BF16, I32 = jnp.float32, jnp.bfloat16, jnp.int32

N_HEADS = 8
DIFF_DH = 64
DSA_DH = 128
IDX_HEADS = 8
IDX_DH = 64
DSA_TOPK = 256
MOBA_DH = 64
MOBA_BLOCK = 256
MOBA_TOPK = 3
MLA_Q_RANK = 512
MLA_KV_RANK = 256
MLA_NOPE = 128
MLA_ROPE = 64
MLA_VDH = 128
ROPE_THETA = 10000.0
REL_BUCKETS = 32
REL_MAX_DIST = 128
N_EXPERTS = 8
PAGE = 128
NEG = -1e30
LN_EPS = 1e-5
INT_MIN = -(2 ** 31)

LANES = 128
TQ = 128
TK = 256
SROWS = 16
CHUNK_PAGES = 16
VMEM_LIMIT = 52 * 1024 * 1024


def _cp(sem, vmem=VMEM_LIMIT):
    return pltpu.CompilerParams(dimension_semantics=sem, vmem_limit_bytes=vmem)


def _tile(n, pref, mult):
    best = None
    for t in range(mult, min(n, pref) + 1, mult):
        if n % t == 0:
            best = t
    return best if best is not None else n


def _mm_kernel(x_ref, w_ref, o_ref):
    o_ref[...] = jnp.dot(x_ref[...], w_ref[...], preferred_element_type=F32).astype(o_ref.dtype)


def _matmul(x, w, out_dtype, name, tn_pref=512):
    m, k = x.shape
    n = w.shape[1]
    tm = _tile(m, 640, 16)
    tn = _tile(n, tn_pref, LANES)
    return pl.pallas_call(
        _mm_kernel, grid=(m // tm, n // tn),
        in_specs=[pl.BlockSpec((tm, k), lambda i, j: (i, 0)), pl.BlockSpec((k, tn), lambda i, j: (0, j))],
        out_specs=pl.BlockSpec((tm, tn), lambda i, j: (i, j)),
        out_shape=jax.ShapeDtypeStruct((m, n), out_dtype),
        compiler_params=_cp(("parallel", "arbitrary")), name=name)(x, w)


def _swiglu_kernel(x_ref, wg_ref, wu_ref, o_ref):
    x = x_ref[...]
    g = jnp.dot(x, wg_ref[...], preferred_element_type=F32)
    u = jnp.dot(x, wu_ref[...], preferred_element_type=F32)
    o_ref[...] = (g * jax.nn.sigmoid(g) * u).astype(o_ref.dtype)


def _swiglu(x, w_in, name):
    m, k = x.shape
    e, _, f2 = w_in.shape
    f = f2 // 2
    tm = _tile(m, 640, 16)
    tn = _tile(f, 512, LANES)
    nper = f // tn
    return pl.pallas_call(
        _swiglu_kernel, grid=(m // tm, e * nper),
        in_specs=[pl.BlockSpec((tm, k), lambda i, j: (i, 0)),
                  pl.BlockSpec((None, k, tn), lambda i, j: (j // nper, 0, j % nper)),
                  pl.BlockSpec((None, k, tn), lambda i, j: (j // nper, 0, j % nper + nper))],
        out_specs=pl.BlockSpec((tm, tn), lambda i, j: (i, j)),
        out_shape=jax.ShapeDtypeStruct((m, e * f), BF16),
        compiler_params=_cp(("parallel", "arbitrary")), name=name)(x, w_in, w_in)


def _ln_kernel(x_ref, h_ref, g_ref, b_ref, o_ref, ob_ref, *, alpha):
    z = alpha * x_ref[...] + h_ref[...]
    mu = jnp.mean(z, axis=-1, keepdims=True)
    zc = z - mu
    var = jnp.mean(zc * zc, axis=-1, keepdims=True)
    y = zc * lax.rsqrt(var + LN_EPS) * g_ref[...] + b_ref[...]
    o_ref[...] = y
    ob_ref[...] = y.astype(BF16)


def _deepnorm_ln(x, h, g, b, alpha, name):
    m, d = x.shape
    tm = _tile(m, 320, 16)
    row = pl.BlockSpec((tm, d), lambda i: (i, 0))
    vec = pl.BlockSpec((1, d), lambda i: (0, 0))
    return pl.pallas_call(
        functools.partial(_ln_kernel, alpha=alpha), grid=(m // tm,),
        in_specs=[row, row, vec, vec], out_specs=[row, row],
        out_shape=[jax.ShapeDtypeStruct((m, d), F32), jax.ShapeDtypeStruct((m, d), BF16)],
        compiler_params=_cp(("parallel",)), name=name)(x, h, g.reshape(1, d), b.reshape(1, d))


def _ple_kernel(xb_ref, x_ref, wg_ref, pb_ref, wp_ref, o_ref, ob_ref):
    gate = jax.nn.sigmoid(jnp.dot(xb_ref[...], wg_ref[...], preferred_element_type=F32))
    proj = jnp.dot(pb_ref[...], wp_ref[...], preferred_element_type=F32)
    y = x_ref[...] + gate * proj
    o_ref[...] = y
    ob_ref[...] = y.astype(BF16)


def _ple(xb, x, wg, pb, wp, name):
    m, d = x.shape
    pd = pb.shape[1]
    tm = _tile(m, 640, 16)
    tn = _tile(d, 512, LANES)
    return pl.pallas_call(
        _ple_kernel, grid=(m // tm, d // tn),
        in_specs=[pl.BlockSpec((tm, d), lambda i, j: (i, 0)), pl.BlockSpec((tm, tn), lambda i, j: (i, j)),
                  pl.BlockSpec((d, tn), lambda i, j: (0, j)), pl.BlockSpec((tm, pd), lambda i, j: (i, 0)),
                  pl.BlockSpec((pd, tn), lambda i, j: (0, j))],
        out_specs=[pl.BlockSpec((tm, tn), lambda i, j: (i, j)), pl.BlockSpec((tm, tn), lambda i, j: (i, j))],
        out_shape=[jax.ShapeDtypeStruct((m, d), F32), jax.ShapeDtypeStruct((m, d), BF16)],
        compiler_params=_cp(("parallel", "arbitrary")), name=name)(xb, x, wg, pb, wp)


def _router_kernel(xb_ref, w_ref, o_ref):
    logits = jnp.dot(xb_ref[...], w_ref[...], preferred_element_type=F32)
    lane = lax.broadcasted_iota(I32, logits.shape, 1)
    logits = jnp.where(lane < N_EXPERTS, logits, -jnp.inf)
    m1 = jnp.max(logits, axis=-1, keepdims=True)
    i1 = jnp.min(jnp.where(logits == m1, lane, LANES), axis=-1, keepdims=True)
    rest = jnp.where(lane == i1, -jnp.inf, logits)
    m2 = jnp.max(rest, axis=-1, keepdims=True)
    i2 = jnp.min(jnp.where(rest == m2, lane, LANES), axis=-1, keepdims=True)
    e2 = jnp.exp(m2 - m1)
    den = 1.0 + e2
    o_ref[...] = jnp.where(lane == 0, i1.astype(F32),
                           jnp.where(lane == 1, i2.astype(F32),
                                     jnp.where(lane == 2, 1.0 / den, jnp.where(lane == 3, e2 / den, 0.0))))


def _router(xb, w, name):
    m, d = xb.shape
    tm = _tile(m, 640, 16)
    return pl.pallas_call(
        _router_kernel, grid=(m // tm,),
        in_specs=[pl.BlockSpec((tm, d), lambda i: (i, 0)), pl.BlockSpec((d, LANES), lambda i: (0, 0))],
        out_specs=pl.BlockSpec((tm, LANES), lambda i: (i, 0)),
        out_shape=jax.ShapeDtypeStruct((m, LANES), F32),
        compiler_params=_cp(("parallel",)), name=name)(xb, w)


MOE_TILE = 512
GATHER_ROWS = 256


def _row_copy(src_hbm, src_row, dst, dst_row, sem):
    return pltpu.make_async_copy(src_hbm.at[pl.ds(src_row, 1)], dst.at[pl.ds(dst_row, 1)], sem)


def _gather_rows_kernel(tok_ref, x_hbm, o_ref, buf, sem):
    s = pl.program_id(0)
    slot = s % 2
    rows = buf.shape[1]

    def start(step, into):
        def issue(r, c):
            _row_copy(x_hbm, tok_ref[step * rows + r], buf.at[into], r, sem.at[into]).start()
            return c
        lax.fori_loop(0, rows, issue, 0)

    @pl.when(s == 0)
    def _():
        start(0, 0)

    @pl.when(s + 1 < pl.num_programs(0))
    def _():
        start(s + 1, 1 - slot)

    def drain(r, c):
        _row_copy(x_hbm, 0, buf.at[slot], r, sem.at[slot]).wait()
        return c

    lax.fori_loop(0, rows, drain, 0)
    o_ref[...] = buf[slot].astype(o_ref.dtype)


def _gather_rows(x, token_of_slot):
    p = token_of_slot.shape[0]
    d = x.shape[1]
    return pl.pallas_call(
        _gather_rows_kernel,
        grid_spec=pltpu.PrefetchScalarGridSpec(
            num_scalar_prefetch=1, grid=(p // GATHER_ROWS,),
            in_specs=[pl.BlockSpec(memory_space=pl.ANY)],
            out_specs=pl.BlockSpec((GATHER_ROWS, d), lambda i, tok: (i, 0)),
            scratch_shapes=[pltpu.VMEM((2, GATHER_ROWS, d), x.dtype), pltpu.SemaphoreType.DMA((2,))]),
        out_shape=jax.ShapeDtypeStruct((p, d), BF16),
        compiler_params=_cp(("arbitrary",)), name="moe_gather")(token_of_slot, x)


def _gswiglu_kernel(te_ref, nu_ref, x_ref, wg_ref, wu_ref, o_ref, wg_scr, wu_scr):
    i = pl.program_id(1)

    @pl.when(jnp.logical_or(i == 0, te_ref[i] != te_ref[jnp.maximum(i - 1, 0)]))
    def _():
        wg_scr[...] = wg_ref[...].astype(BF16)
        wu_scr[...] = wu_ref[...].astype(BF16)

    @pl.when(i < nu_ref[0])
    def _():
        x = x_ref[...]
        g = jnp.dot(x, wg_scr[...], preferred_element_type=F32)
        u = jnp.dot(x, wu_scr[...], preferred_element_type=F32)
        o_ref[...] = (g * jax.nn.sigmoid(g) * u).astype(o_ref.dtype)

    @pl.when(i >= nu_ref[0])
    def _():
        o_ref[...] = jnp.zeros_like(o_ref)


def _grouped_swiglu(xs, w_in, layer, tile_expert, n_used):
    p, k = xs.shape
    f = w_in.shape[3] // 2
    tn = _tile(f, 512, LANES)
    nper = f // tn
    return pl.pallas_call(
        _gswiglu_kernel,
        grid_spec=pltpu.PrefetchScalarGridSpec(
            num_scalar_prefetch=2, grid=(nper, p // MOE_TILE),
            in_specs=[pl.BlockSpec((MOE_TILE, k), lambda j, i, te, nu: (i, 0)),
                      pl.BlockSpec((None, None, k, tn), lambda j, i, te, nu: (layer, te[i], 0, j)),
                      pl.BlockSpec((None, None, k, tn), lambda j, i, te, nu: (layer, te[i], 0, j + nper))],
            out_specs=pl.BlockSpec((MOE_TILE, tn), lambda j, i, te, nu: (i, j)),
            scratch_shapes=[pltpu.VMEM((k, tn), BF16), pltpu.VMEM((k, tn), BF16)]),
        out_shape=jax.ShapeDtypeStruct((p, f), BF16),
        compiler_params=_cp(("arbitrary", "arbitrary")), name="moe_in")(tile_expert, n_used, xs, w_in, w_in)


def _gmm_kernel(te_ref, nu_ref, h_ref, w_ref, o_ref):
    @pl.when(pl.program_id(1) < nu_ref[0])
    def _():
        o_ref[...] = jnp.dot(h_ref[...], w_ref[...], preferred_element_type=F32)

    @pl.when(pl.program_id(1) >= nu_ref[0])
    def _():
        o_ref[...] = jnp.zeros_like(o_ref)


def _grouped_matmul(h, w, layer, tile_expert, n_used):
    p, f = h.shape
    n = w.shape[3]
    tn = _tile(n, 512, LANES)
    return pl.pallas_call(
        _gmm_kernel,
        grid_spec=pltpu.PrefetchScalarGridSpec(
            num_scalar_prefetch=2, grid=(n // tn, p // MOE_TILE),
            in_specs=[pl.BlockSpec((MOE_TILE, f), lambda j, i, te, nu: (i, 0)),
                      pl.BlockSpec((None, None, f, tn), lambda j, i, te, nu: (layer, te[i], 0, j))],
            out_specs=pl.BlockSpec((MOE_TILE, tn), lambda j, i, te, nu: (i, j))),
        out_shape=jax.ShapeDtypeStruct((p, n), F32),
        compiler_params=_cp(("arbitrary", "arbitrary")), name="moe_out")(tile_expert, n_used, h, w)


def _moe_combine_ln_kernel(s1_ref, s2_ref, route_ref, x_ref, ys_hbm, g_ref, b_ref, o_ref, ob_ref,
                           buf1, buf2, sem, *, alpha):
    tm = buf1.shape[0]
    base = pl.program_id(0) * tm

    def issue(r, c):
        _row_copy(ys_hbm, s1_ref[base + r], buf1, r, sem).start()
        _row_copy(ys_hbm, s2_ref[base + r], buf2, r, sem).start()
        return c

    def drain(r, c):
        _row_copy(ys_hbm, 0, buf1, r, sem).wait()
        _row_copy(ys_hbm, 0, buf2, r, sem).wait()
        return c

    lax.fori_loop(0, tm, issue, 0)
    lax.fori_loop(0, tm, drain, 0)
    f = route_ref[:, 2:3] * buf1[...] + route_ref[:, 3:4] * buf2[...]
    z = alpha * x_ref[...] + f
    mu = jnp.mean(z, axis=-1, keepdims=True)
    zc = z - mu
    var = jnp.mean(zc * zc, axis=-1, keepdims=True)
    y = zc * lax.rsqrt(var + LN_EPS) * g_ref[...] + b_ref[...]
    o_ref[...] = y
    ob_ref[...] = y.astype(BF16)


def _moe_combine_ln(ys, slot1, slot2, route, x, g, b, alpha):
    m, d = x.shape
    tm = _tile(m, 320, 16)
    row = lambda i, s1, s2: (i, 0)
    vec = pl.BlockSpec((1, d), lambda i, s1, s2: (0, 0))
    return pl.pallas_call(
        functools.partial(_moe_combine_ln_kernel, alpha=alpha),
        grid_spec=pltpu.PrefetchScalarGridSpec(
            num_scalar_prefetch=2, grid=(m // tm,),
            in_specs=[pl.BlockSpec((tm, LANES), row), pl.BlockSpec((tm, d), row),
                      pl.BlockSpec(memory_space=pl.ANY), vec, vec],
            out_specs=[pl.BlockSpec((tm, d), row), pl.BlockSpec((tm, d), row)],
            scratch_shapes=[pltpu.VMEM((tm, d), F32), pltpu.VMEM((tm, d), F32), pltpu.SemaphoreType.DMA(())]),
        out_shape=[jax.ShapeDtypeStruct((m, d), F32), jax.ShapeDtypeStruct((m, d), BF16)],
        compiler_params=_cp(("arbitrary",)), name="moe_combine_ln")(
            slot1, slot2, route, x, ys, g.reshape(1, d), b.reshape(1, d))


def _dispatch_plan(route, n_tiles):
    m = route.shape[0]
    e = jnp.concatenate([route[:, 0], route[:, 1]]).astype(I32)
    onehot = (e[:, None] == jnp.arange(N_EXPERTS)[None, :]).astype(I32)
    pos = jnp.sum((jnp.cumsum(onehot, axis=0) - 1) * onehot, axis=1)
    counts = jnp.sum(onehot, axis=0)
    padded = (counts + MOE_TILE - 1) // MOE_TILE * MOE_TILE
    ends = jnp.cumsum(padded)
    slot = (ends - padded)[e] + pos
    token = jnp.tile(jnp.arange(m, dtype=I32), 2)
    token_of_slot = jnp.zeros((n_tiles * MOE_TILE,), I32).at[slot].set(token)
    n_used = (ends[-1] // MOE_TILE).astype(I32)
    tile_start = jnp.arange(n_tiles, dtype=I32) * MOE_TILE
    tile_expert = jnp.sum((tile_start[:, None] >= ends[None, :]).astype(I32), axis=1)
    last = jnp.sum((ends[-1] - 1 >= ends).astype(I32))
    tile_expert = jnp.minimum(tile_expert, last).astype(I32)
    return slot[:m].astype(I32), slot[m:].astype(I32), token_of_slot, tile_expert, n_used.reshape(1)


def _t5_bucket(d):
    n = jnp.maximum(d, 0)
    max_exact = REL_BUCKETS // 2
    nf = jnp.maximum(n, 1).astype(F32)
    large = max_exact + (jnp.log(nf / max_exact) / math.log(REL_MAX_DIST / max_exact)
                         * (REL_BUCKETS - max_exact)).astype(I32)
    large = jnp.minimum(large, REL_BUCKETS - 1)
    return jnp.where(n < max_exact, n, large)


def _bias_rows(bucket, rb_ref, h):
    acc = jnp.zeros(bucket.shape, F32)
    for b in range(REL_BUCKETS):
        acc = jnp.where(bucket == b, rb_ref[b, h], acc)
    return acc


def _tbl_prompt_kernel(rb_ref, o_ref, *, hb):
    t = pl.program_id(0)
    off = (t // 2) * TK + (t % 2) * TQ
    k = lax.broadcasted_iota(I32, (TK, TQ), 0)
    q = lax.broadcasted_iota(I32, (TK, TQ), 1)
    d = off + q - k
    bucket = _t5_bucket(d)
    for h in range(hb):
        o_ref[0, :, h * TQ:(h + 1) * TQ] = jnp.where(d >= 0, _bias_rows(bucket, rb_ref, h), NEG)


def _bias_tables_prompt(rb):
    hb = rb.shape[1]
    return pl.pallas_call(
        functools.partial(_tbl_prompt_kernel, hb=hb), grid=(6,),
        in_specs=[pl.BlockSpec(memory_space=pltpu.SMEM)],
        out_specs=pl.BlockSpec((1, TK, hb * TQ), lambda t: (t, 0, 0)),
        out_shape=jax.ShapeDtypeStruct((6, TK, hb * TQ), F32),
        compiler_params=_cp(("arbitrary",)), name="bias_tables_prompt")(rb)


def _tbl_sample_kernel(rb_ref, o_ref, *, past):
    k = lax.broadcasted_iota(I32, (1, o_ref.shape[1]), 1)
    bucket = _t5_bucket(past - k)
    for h in range(N_HEADS):
        row = _bias_rows(bucket, rb_ref, h)
        o_ref[h:h + 1, :] = row
        o_ref[N_HEADS + h:N_HEADS + h + 1, :] = row


def _bias_table_sample(rb, past):
    w = past + LANES
    return pl.pallas_call(
        functools.partial(_tbl_sample_kernel, past=past),
        in_specs=[pl.BlockSpec(memory_space=pltpu.SMEM)],
        out_specs=pl.BlockSpec(memory_space=pltpu.VMEM),
        out_shape=jax.ShapeDtypeStruct((SROWS, w), F32),
        compiler_params=_cp(None), name="bias_table_sample")(rb)


def _topk_additive(score, key_ref, k, axis=1):
    n = score.shape[axis]
    stat_shape = (score.shape[0], 1) if axis == 1 else (1, score.shape[1])
    score = jnp.where(score == 0.0, 0.0, score)
    bits = pltpu.bitcast(score, I32)
    key_ref[...] = jnp.where(bits < 0, bits ^ 0x7FFFFFFF, bits)
    kf = float(k)

    slabs = 8 if (axis == 0 and n % (8 * 8) == 0) else 1

    def count(cond):
        ones = jnp.where(cond, 1.0, 0.0)
        if slabs == 1:
            return jnp.sum(ones, axis=axis, keepdims=True)
        step = n // slabs
        parts = [jnp.sum(ones[s * step:(s + 1) * step], axis=0, keepdims=True) for s in range(slabs)]
        while len(parts) > 1:
            parts = [parts[a] + parts[a + 1] for a in range(0, len(parts), 2)]
        return parts[0]

    t0 = jnp.where(count(key_ref[...] >= 0) >= kf, 0, INT_MIN).astype(I32)

    def value_bit(it, t):
        cand = t + lax.shift_left(jnp.int32(1), 30 - it)
        return jnp.where(count(key_ref[...] >= cand) >= kf, cand, t)

    t = lax.fori_loop(0, 31, value_bit, t0)
    need = kf - count(key_ref[...] > t)
    tied = count(key_ref[...] == t) > need
    idx_bits = max(1, (n - 1).bit_length())
    trips = (jnp.max(jnp.where(tied, 1.0, 0.0)) > 0.0).astype(I32) * idx_bits

    def index_bit(it, x):
        cand = x + lax.shift_left(jnp.int32(1), idx_bits - 1 - it)
        idx = lax.broadcasted_iota(I32, score.shape, axis)
        below = count(jnp.logical_and(key_ref[...] == t, idx < cand))
        return jnp.where(below < need, cand, x)

    x = lax.fori_loop(0, trips, index_bit, jnp.zeros(stat_shape, I32))
    key = key_ref[...]
    idx = lax.broadcasted_iota(I32, score.shape, axis)
    keep_eq = jnp.logical_or(jnp.logical_not(tied), idx <= x)
    sel = jnp.logical_or(key > t, jnp.logical_and(key == t, keep_eq))
    return jnp.where(sel, 0.0, NEG)


def _idx_prompt_kernel(iqt_ref, kx_ref, wt_ref, o_ref, key_ref, *, topk):
    i = pl.program_id(1)
    lk = kx_ref.shape[1]
    ik = kx_ref[0][:, 0:IDX_DH].astype(BF16)
    score = jnp.zeros((lk, TQ), F32)
    for h in range(IDX_HEADS):
        qh = iqt_ref[0, h * IDX_DH:(h + 1) * IDX_DH, :].astype(BF16)
        sc = jnp.maximum(jnp.dot(ik, qh, preferred_element_type=F32) * IDX_DH ** -0.5, 0.0)
        score = score + sc * (wt_ref[0, h:h + 1, :] * IDX_HEADS ** -0.5)
    kpos = lax.broadcasted_iota(I32, (lk, TQ), 0)
    qpos = i * TQ + lax.broadcasted_iota(I32, (lk, TQ), 1)
    score = jnp.where(kpos <= qpos, score, NEG)
    o_ref[0] = _topk_additive(score, key_ref, topk, axis=0).astype(BF16)


def _idx_prompt(iq_t, h3, iw_t, topk):
    b, l, _ = h3.shape
    return pl.pallas_call(
        functools.partial(_idx_prompt_kernel, topk=topk), grid=(b, l // TQ),
        in_specs=[pl.BlockSpec((1, IDX_HEADS * IDX_DH, TQ), lambda bi, i: (bi, 0, i)),
                  pl.BlockSpec((1, l, LANES), lambda bi, i: (bi, 0, 24)),
                  pl.BlockSpec((1, IDX_HEADS, TQ), lambda bi, i: (bi, 0, i))],
        out_specs=pl.BlockSpec((1, l, TQ), lambda bi, i: (bi, 0, i)),
        out_shape=jax.ShapeDtypeStruct((b, l, l), BF16),
        scratch_shapes=[pltpu.VMEM((l, TQ), I32)],
        compiler_params=_cp(("parallel", "arbitrary")), name="dsa_index_prompt")(iq_t, h3, iw_t)


def _flash_t_kernel(*refs, variant, scale, n_sel, lam_init):
    if variant == "A":
        qt_ref, k_ref, vt_ref, tbl_ref, lam_ref, g_ref, o_ref, qt_scr, s_scr, p_scr, m_scr, l_scr, a_scr, acc_scr = refs
    elif variant == "B":
        qt_ref, k_ref, vt_ref, tbl_ref, mask_ref, o_ref, qt_scr, s_scr, p_scr, m_scr, l_scr, a_scr, acc_scr = refs
    elif variant == "C":
        (qt_ref, k_ref, vt_ref, tbl_ref, o_ref, qt_scr, s_scr, p_scr, m_scr, l_scr, a_scr, acc_scr,
         sel_scr, mean_scr) = refs
    else:
        qt_ref, k_ref, vt_ref, tbl_ref, o_ref, qt_scr, s_scr, p_scr, m_scr, l_scr, a_scr, acc_scr = refs
    i = pl.program_id(1)
    dk, cols_total = qt_scr.shape
    groups = cols_total // TQ
    hb = tbl_ref.shape[2] // TQ
    own = (i * TQ) // TK
    parity = ((i * TQ) % TK) // TQ

    if variant == "A":
        feat = lax.broadcasted_iota(I32, (LANES, TQ), 0)
        for c in range(2):
            for h in range(N_HEADS):
                slab = qt_ref[0, h * LANES:(h + 1) * LANES, :]
                keep = (feat < DIFF_DH) if c == 0 else (feat >= DIFF_DH)
                g = c * N_HEADS + h
                qt_scr[:, g * TQ:(g + 1) * TQ] = jnp.where(keep, slab, 0.0).astype(BF16)
    else:
        for h in range(N_HEADS):
            qt_scr[:, h * TQ:(h + 1) * TQ] = qt_ref[0, h * dk:(h + 1) * dk, :].astype(BF16)
    m_scr[...] = jnp.full(m_scr.shape, NEG, F32)
    l_scr[...] = jnp.zeros(l_scr.shape, F32)
    acc_scr[...] = jnp.zeros(acc_scr.shape, F32)

    if variant == "C":
        nblk = k_ref.shape[1] // TK
        nb_pad = mean_scr.shape[0]
        mean_scr[...] = jnp.zeros(mean_scr.shape, F32)
        for n in range(nblk):
            mean_scr[n:n + 1, :] = jnp.sum(k_ref[0, n * TK:(n + 1) * TK, :], axis=0, keepdims=True) * (1.0 / TK)
        nrow = lax.broadcasted_iota(I32, (nb_pad, TQ), 0)
        for h in range(N_HEADS):
            gate = jnp.dot(mean_scr[...], qt_ref[0, h * LANES:(h + 1) * LANES, :],
                           preferred_element_type=F32)
            rank = jnp.zeros((nb_pad, TQ), F32)
            for n2 in range(nblk):
                other = gate[n2:n2 + 1, :]
                beats = jnp.where(other > gate, 1.0,
                                  jnp.where(other == gate, jnp.where(n2 < nrow, 1.0, 0.0), 0.0))
                rank = rank + beats * (n2 < own).astype(F32)
            add = jnp.where(nrow < own, jnp.where(rank < n_sel, 0.0, NEG), 0.0)
            for n in range(nblk):
                sel_scr[n, :, h * TQ:(h + 1) * TQ] = add[n:n + 1, :]

    def step(j, carry):
        off = pl.multiple_of(j * TK, TK)
        kb = k_ref[0, pl.ds(off, TK), :].astype(BF16)
        s_scr[...] = jnp.dot(kb, qt_scr[...], preferred_element_type=F32)
        t = jnp.minimum(own - j, 2) * 2 + parity
        for g in range(groups):
            cols = slice(g * TQ, (g + 1) * TQ)
            hcol = (g % hb) * TQ
            s = s_scr[:, cols] * scale + tbl_ref[t, :, hcol:hcol + TQ]
            if variant == "B":
                s = s + mask_ref[0, pl.ds(off, TK), :].astype(F32)
            if variant == "C":
                s = s + sel_scr[j, :, cols]
            m_prev = m_scr[:, cols]
            m_new = jnp.maximum(m_prev, jnp.max(s, axis=0, keepdims=True))
            alpha = jnp.exp(m_prev - m_new)
            p = jnp.exp(s - m_new)
            l_scr[:, cols] = alpha * l_scr[:, cols] + jnp.sum(p, axis=0, keepdims=True)
            m_scr[:, cols] = m_new
            a_scr[:, cols] = alpha
            p_scr[:, cols] = p.astype(BF16)
        vtb = vt_ref[0, :, pl.ds(off, TK)].astype(BF16)
        acc_scr[...] = acc_scr[...] * a_scr[...] + jnp.dot(vtb, p_scr[...], preferred_element_type=F32)
        return carry

    lax.fori_loop(0, own + 1, step, 0)

    def head_out(g):
        cols = slice(g * TQ, (g + 1) * TQ)
        return jnp.transpose(acc_scr[:, cols] / l_scr[:, cols])

    dv = acc_scr.shape[0]
    lane = lax.broadcasted_iota(I32, (TQ, LANES), 1)
    if variant == "A":
        lv = lam_ref[...]
        lam = (jnp.exp(jnp.sum(lv[0:1] * lv[1:2], axis=-1, keepdims=True))
               - jnp.exp(jnp.sum(lv[2:3] * lv[3:4], axis=-1, keepdims=True)) + lam_init)
        for h in range(N_HEADS):
            o = head_out(h) - lam * head_out(N_HEADS + h)
            o = o * lax.rsqrt(jnp.mean(o * o, axis=-1, keepdims=True) + LN_EPS)
            o_ref[0, :, h * LANES:(h + 1) * LANES] = (o * g_ref[...] * (1.0 - lam_init)).astype(o_ref.dtype)
    elif variant == "C":
        for pair in range(N_HEADS // 2):
            a0 = pltpu.roll(head_out(2 * pair), MOBA_DH, 1)
            a1 = head_out(2 * pair + 1)
            o_ref[0, :, pair * LANES:(pair + 1) * LANES] = jnp.where(lane < MOBA_DH, a0, a1).astype(o_ref.dtype)
    else:
        for h in range(N_HEADS):
            o_ref[0, :, h * dv:(h + 1) * dv] = head_out(h).astype(o_ref.dtype)


def _flash_t(variant, qt, k3, vt, tbl, *, k_blk, k_width, out_width, groups, scale,
             mask_t=None, lam=None, subln_g=None, lam_init=0.0, n_sel=0, name):
    b, wq, l = qt.shape
    dv = vt.shape[1]
    cols = groups * TQ
    in_specs = [pl.BlockSpec((1, wq, TQ), lambda bi, i: (bi, 0, i)),
                pl.BlockSpec((1, l, k_width), lambda bi, i: (bi, 0, k_blk)),
                pl.BlockSpec((1, dv, l), lambda bi, i: (bi, 0, 0)),
                pl.BlockSpec(tbl.shape, lambda bi, i: (0, 0, 0))]
    args = [qt, k3, vt, tbl]
    if variant == "A":
        in_specs += [pl.BlockSpec(lam.shape, lambda bi, i: (0, 0)), pl.BlockSpec((1, LANES), lambda bi, i: (0, 0))]
        args += [lam, subln_g.reshape(1, LANES)]
    if variant == "B":
        in_specs.append(pl.BlockSpec((1, l, TQ), lambda bi, i: (bi, 0, i)))
        args.append(mask_t)
    scratch = [pltpu.VMEM((k_width, cols), BF16), pltpu.VMEM((TK, cols), F32), pltpu.VMEM((TK, cols), BF16),
               pltpu.VMEM((1, cols), F32), pltpu.VMEM((1, cols), F32), pltpu.VMEM((1, cols), F32),
               pltpu.VMEM((dv, cols), F32)]
    if variant == "C":
        nb_pad = max(8, l // TK)
        scratch += [pltpu.VMEM((nb_pad, 1, cols), F32), pltpu.VMEM((nb_pad, LANES), F32)]
    kern = functools.partial(_flash_t_kernel, variant=variant, scale=scale, n_sel=n_sel, lam_init=lam_init)
    return pl.pallas_call(
        kern, grid=(b, l // TQ), in_specs=in_specs,
        out_specs=pl.BlockSpec((1, TQ, out_width), lambda bi, i: (bi, i, 0)),
        out_shape=jax.ShapeDtypeStruct((b, l, out_width), BF16),
        scratch_shapes=scratch, compiler_params=_cp(("parallel", "arbitrary")), name=name)(*args)


def _page_copy(pool_ref, pt_ref, buf, sem, layer, seq, slot, row, page):
    return pltpu.make_async_copy(pool_ref.at[pt_ref[seq, page], layer], buf.at[slot, row], sem.at[slot])


def _fetch_pages(pool_ref, pt_ref, buf, sem, layer, n_pages, seqs=1):
    s = pl.program_id(0)
    slot = s % 2

    def start(step, into):
        def one_seq(b, c):
            for p in range(n_pages):
                _page_copy(pool_ref, pt_ref, buf, sem, layer, step * seqs + b, into, b * n_pages + p, p).start()
            return c
        lax.fori_loop(0, seqs, one_seq, 0)

    @pl.when(s == 0)
    def _():
        start(0, 0)

    @pl.when(s + 1 < pl.num_programs(0))
    def _():
        start(s + 1, 1 - slot)

    def wait_seq(b, c):
        for p in range(n_pages):
            _page_copy(pool_ref, pt_ref, buf, sem, layer, 0, slot, b * n_pages + p, p).wait()
        return c
    lax.fori_loop(0, seqs, wait_seq, 0)
    return slot


def _chunk_t(buf, slot, c):
    pages = [buf[slot, c * CHUNK_PAGES + p] for p in range(CHUNK_PAGES)]
    return jnp.concatenate(pages, axis=1).astype(BF16)


def _sample_attn_kernel(*refs, variant, layer, pieces, v_lo, v_hi, scale, n_sel, transposed):
    if variant == "B":
        pt_ref, q_ref, pool_ref, new_ref, tb_ref, ex_ref, o_ref, buf, sem, s_scr, p_scr = refs
    elif variant == "C":
        pt_ref, q_ref, pool_ref, new_ref, tb_ref, o_ref, buf, sem, s_scr, p_scr, mean_scr, e_scr = refs
    else:
        pt_ref, q_ref, pool_ref, new_ref, tb_ref, o_ref, buf, sem, s_scr, p_scr = refs
    n_pages = buf.shape[1]
    past = n_pages * PAGE
    ckeys = CHUNK_PAGES * PAGE
    n_chunks = n_pages // CHUNK_PAGES
    slot = _fetch_pages(pool_ref, pt_ref, buf, sem, layer, n_pages)

    qf = q_ref[0]
    qb = qf.astype(BF16)

    if variant == "C":
        nblk = past // MOBA_BLOCK

        @pl.when(pl.program_id(0) == 0)
        def _():
            blk = lax.broadcasted_iota(I32, e_scr.shape, 0)
            key = lax.broadcasted_iota(I32, e_scr.shape, 1)
            e_scr[...] = jnp.where(key // MOBA_BLOCK == blk, 1.0, 0.0).astype(BF16)

    for c in range(n_chunks):
        s = None
        if transposed:
            kt = _chunk_t(buf, slot, c)
            for lo, hi in pieces:
                part = jnp.dot(qb[:, lo:hi], kt[lo:hi, :], preferred_element_type=F32)
                s = part if s is None else s + part
        else:
            kc = buf[slot, c * CHUNK_PAGES:(c + 1) * CHUNK_PAGES].reshape(ckeys, buf.shape[3])
            for lo, hi in pieces:
                part = lax.dot_general(qb[:, lo:hi], kc[:, lo:hi].astype(BF16), (((1,), (1,)), ((), ())),
                                       preferred_element_type=F32)
                s = part if s is None else s + part
        s_scr[:, c * ckeys:(c + 1) * ckeys] = s
        if variant == "C":
            per = ckeys // MOBA_BLOCK
            for n in range(per):
                mean_scr[c * per + n:c * per + n + 1, :] = (
                    jnp.sum(kc[n * MOBA_BLOCK:(n + 1) * MOBA_BLOCK, :], axis=0, keepdims=True) * (1.0 / MOBA_BLOCK))

    s = s_scr[...] * scale
    new = new_ref[0]
    s_new = None
    for lo, hi in pieces:
        part = jnp.sum(qf[:, lo:hi] * new[:, lo:hi], axis=-1, keepdims=True)
        s_new = part if s_new is None else s_new + part
    s_new = s_new * scale
    if tb_ref is not None:
        s = s + tb_ref[:, 0:past]
        s_new = s_new + tb_ref[:, past:past + 1]
    if variant == "B":
        s = s + ex_ref[0, :, 0:past]
        s_new = s_new + ex_ref[0, :, past:past + 1]
    if variant == "C":
        gate = lax.dot_general(qf, mean_scr[...], (((1,), (1,)), ((), ())), preferred_element_type=F32)
        nlane = lax.broadcasted_iota(I32, gate.shape, 1)
        rank = jnp.zeros(gate.shape, F32)
        for n2 in range(nblk):
            col = gate[:, n2:n2 + 1]
            rank = rank + jnp.where(col > gate, 1.0,
                                    jnp.where(col == gate, jnp.where(n2 < nlane, 1.0, 0.0), 0.0))
        chosen = jnp.where(rank < n_sel, 1.0, 0.0).astype(BF16)
        keep = jnp.dot(chosen, e_scr[...], preferred_element_type=F32)
        s = s + (keep - 1.0) * (-NEG)

    m = jnp.maximum(jnp.max(s, axis=-1, keepdims=True), s_new)
    p = jnp.exp(s - m)
    p_new = jnp.exp(s_new - m)
    denom = jnp.sum(p, axis=-1, keepdims=True) + p_new
    p_scr[...] = p.astype(BF16)
    acc = p_new * new[:, v_lo:v_hi]
    for c in range(n_chunks):
        pc = p_scr[:, c * ckeys:(c + 1) * ckeys]
        if transposed:
            vt = _chunk_t(buf, slot, c)[v_lo:v_hi, :]
            acc = acc + lax.dot_general(pc, vt, (((1,), (1,)), ((), ())), preferred_element_type=F32)
        else:
            vc = buf[slot, c * CHUNK_PAGES:(c + 1) * CHUNK_PAGES].reshape(ckeys, buf.shape[3])[:, v_lo:v_hi]
            acc = acc + jnp.dot(pc, vc.astype(BF16), preferred_element_type=F32)
    o_ref[0] = acc / denom


def _sample_attn(variant, q, pool, layer, page_table, new_rows, tb, *, pieces, v_lo, v_hi, scale,
                 extra=None, n_sel=0, transposed=False, name):
    ns, _, dq = q.shape
    n_pages = page_table.shape[1]
    past = n_pages * PAGE
    w = new_rows.shape[2]
    page_shape = (w, PAGE) if transposed else (PAGE, w)
    in_specs = [pl.BlockSpec((1, SROWS, dq), lambda s, pt: (s, 0, 0)),
                pl.BlockSpec(memory_space=pl.ANY),
                pl.BlockSpec((1, 1, w), lambda s, pt: (s, 0, 0))]
    args = [q, pool, new_rows]
    if tb is not None:
        in_specs.append(pl.BlockSpec(tb.shape, lambda s, pt: (0, 0)))
        args.append(tb)
    if variant == "B":
        in_specs.append(pl.BlockSpec((1, 1, past + LANES), lambda s, pt: (s, 0, 0)))
        args.append(extra)
    scratch = [pltpu.VMEM((2, n_pages) + page_shape, F32), pltpu.SemaphoreType.DMA((2,)),
               pltpu.VMEM((SROWS, past), F32), pltpu.VMEM((SROWS, past), BF16)]
    if variant == "C":
        nblk = past // MOBA_BLOCK
        scratch += [pltpu.VMEM((nblk, w), F32), pltpu.VMEM((nblk, past), BF16)]
    kern = functools.partial(_sample_attn_kernel, variant=variant, layer=layer, pieces=pieces,
                             v_lo=v_lo, v_hi=v_hi, scale=scale, n_sel=n_sel, transposed=transposed)
    if tb is None:
        inner = kern

        def kern(pt_ref, q_ref, pool_ref, new_ref, *rest):
            return inner(pt_ref, q_ref, pool_ref, new_ref, None, *rest)

    return pl.pallas_call(
        kern,
        grid_spec=pltpu.PrefetchScalarGridSpec(
            num_scalar_prefetch=1, grid=(ns,), in_specs=in_specs,
            out_specs=pl.BlockSpec((1, SROWS, v_hi - v_lo), lambda s, pt: (s, 0, 0)),
            scratch_shapes=scratch),
        out_shape=jax.ShapeDtypeStruct((ns, SROWS, v_hi - v_lo), F32),
        compiler_params=_cp(("arbitrary",)), name=name)(page_table, *args)


IDX_SEQS = 8


def _idx_sample_kernel(pt_ref, iq_ref, w_ref, pool_ref, new_ref, o_ref, buf, sem, row_scr, key_scr, *,
                       layer, topk, n_pages):
    past = n_pages * PAGE
    ckeys = CHUNK_PAGES * PAGE
    slot = _fetch_pages(pool_ref, pt_ref, buf, sem, layer, n_pages, seqs=IDX_SEQS)
    lane = lax.broadcasted_iota(I32, (1, LANES), 1)
    for b in range(IDX_SEQS):
        iqf = iq_ref[b]
        iqb = iqf.astype(BF16)
        wh = w_ref[b][:, 0:1] * IDX_HEADS ** -0.5
        for c in range(n_pages // CHUNK_PAGES):
            pages = [buf[slot, b * n_pages + c * CHUNK_PAGES + p] for p in range(CHUNK_PAGES)]
            kt = jnp.concatenate(pages, axis=1).astype(BF16)
            sc = jnp.maximum(jnp.dot(iqb, kt, preferred_element_type=F32) * IDX_DH ** -0.5, 0.0)
            row_scr[b:b + 1, c * ckeys:(c + 1) * ckeys] = jnp.sum(sc * wh, axis=0, keepdims=True)
        sc_new = jnp.maximum(jnp.sum(iqf * new_ref[b], axis=-1, keepdims=True) * IDX_DH ** -0.5, 0.0)
        score_new = jnp.sum(sc_new * wh, axis=0, keepdims=True)
        row_scr[b:b + 1, past:past + LANES] = jnp.where(lane == 0, score_new, -3e38)
    o_ref[0] = _topk_additive(row_scr[...], key_scr, topk)


def _idx_sample(iq, iw, pool_t, layer, page_table, new_rows, topk):
    ns = iq.shape[0]
    n_pages = page_table.shape[1]
    past = n_pages * PAGE
    wide = past + LANES
    out = pl.pallas_call(
        functools.partial(_idx_sample_kernel, layer=layer, topk=topk, n_pages=n_pages),
        grid_spec=pltpu.PrefetchScalarGridSpec(
            num_scalar_prefetch=1, grid=(ns // IDX_SEQS,),
            in_specs=[pl.BlockSpec((IDX_SEQS, SROWS, IDX_DH), lambda s, pt: (s, 0, 0)),
                      pl.BlockSpec((IDX_SEQS, SROWS, LANES), lambda s, pt: (s, 0, 0)),
                      pl.BlockSpec(memory_space=pl.ANY),
                      pl.BlockSpec((IDX_SEQS, 1, IDX_DH), lambda s, pt: (s, 0, 0))],
            out_specs=pl.BlockSpec((1, IDX_SEQS, wide), lambda s, pt: (s, 0, 0)),
            scratch_shapes=[pltpu.VMEM((2, IDX_SEQS * n_pages, IDX_DH, PAGE), F32), pltpu.SemaphoreType.DMA((2,)),
                            pltpu.VMEM((IDX_SEQS, wide), F32), pltpu.VMEM((IDX_SEQS, wide), I32)]),
        out_shape=jax.ShapeDtypeStruct((ns // IDX_SEQS, IDX_SEQS, wide), F32),
        compiler_params=_cp(("arbitrary",)), name="dsa_index_sample")(page_table, iq, iw, pool_t, new_rows)
    return out.reshape(ns, 1, wide)


def _diff_post_kernel(o_ref, lam_ref, g_ref, out_ref, *, lam_init):
    lv = lam_ref[...]
    lam = (jnp.exp(jnp.sum(lv[0:1] * lv[1:2], axis=-1, keepdims=True))
           - jnp.exp(jnp.sum(lv[2:3] * lv[3:4], axis=-1, keepdims=True)) + lam_init)
    o = o_ref[...]
    d = o[:, 0:N_HEADS, :] - lam * o[:, N_HEADS:2 * N_HEADS, :]
    d = d * lax.rsqrt(jnp.mean(d * d, axis=-1, keepdims=True) + LN_EPS)
    out_ref[...] = d * g_ref[...] * (1.0 - lam_init)


def _diff_post(o, lam, g, lam_init):
    ns = o.shape[0]
    vm = pl.BlockSpec(memory_space=pltpu.VMEM)
    return pl.pallas_call(
        functools.partial(_diff_post_kernel, lam_init=lam_init), in_specs=[vm, vm, vm], out_specs=vm,
        out_shape=jax.ShapeDtypeStruct((ns, N_HEADS, LANES), F32),
        compiler_params=_cp(None), name="diff_post_sample")(o, lam, g.reshape(1, LANES))


def _rope_group(x, cos, sin):
    lane = lax.broadcasted_iota(I32, x.shape, 1)
    half = MLA_ROPE // 2
    other = jnp.where(lane < half, pltpu.roll(x, LANES - half, 1), pltpu.roll(x, half, 1))
    return x * cos + other * sin


def _mla_prep_kernel(h_ref, qg_ref, kg_ref, cos_ref, sin_ref, qn_ref, kv_ref):
    qa = h_ref[:, 0:MLA_Q_RANK]
    qn = qa * lax.rsqrt(jnp.mean(qa * qa, axis=-1, keepdims=True) + LN_EPS) * qg_ref[...]
    qn_ref[...] = qn.astype(BF16)
    kva = h_ref[:, MLA_Q_RANK:MLA_Q_RANK + MLA_KV_RANK]
    kv_ref[:, 0:MLA_KV_RANK] = kva * lax.rsqrt(jnp.mean(kva * kva, axis=-1, keepdims=True) + LN_EPS) * kg_ref[...]
    kr = h_ref[:, MLA_Q_RANK + MLA_KV_RANK + LANES:MLA_Q_RANK + MLA_KV_RANK + 2 * LANES]
    kv_ref[:, MLA_KV_RANK:MLA_KV_RANK + LANES] = _rope_group(kr, cos_ref[...], sin_ref[...])


def _mla_prep(h, qg, kg, cos, sin):
    m = h.shape[0]
    tm = _tile(m, 640, 16)
    wide = MLA_Q_RANK + MLA_KV_RANK + 2 * LANES
    return pl.pallas_call(
        _mla_prep_kernel, grid=(m // tm,),
        in_specs=[pl.BlockSpec((tm, wide), lambda i: (i, 1)),
                  pl.BlockSpec((1, MLA_Q_RANK), lambda i: (0, 0)), pl.BlockSpec((1, MLA_KV_RANK), lambda i: (0, 0)),
                  pl.BlockSpec((tm, LANES), lambda i: (i, 0)), pl.BlockSpec((tm, LANES), lambda i: (i, 0))],
        out_specs=[pl.BlockSpec((tm, MLA_Q_RANK), lambda i: (i, 0)),
                   pl.BlockSpec((tm, MLA_KV_RANK + LANES), lambda i: (i, 0))],
        out_shape=[jax.ShapeDtypeStruct((m, MLA_Q_RANK), BF16),
                   jax.ShapeDtypeStruct((m, MLA_KV_RANK + LANES), F32)],
        compiler_params=_cp(("parallel",)), name="mla_prep")(
            h, qg.reshape(1, -1), kg.reshape(1, -1), cos, sin)


def _mla_q_kernel(qf_ref, wuk_ref, cos_ref, sin_ref, o_ref):
    width = MLA_KV_RANK + LANES
    for h in range(N_HEADS):
        nope = qf_ref[:, h * LANES:(h + 1) * LANES].astype(BF16)
        o_ref[:, h * width:h * width + MLA_KV_RANK] = jnp.dot(
            nope, wuk_ref[h], preferred_element_type=F32).astype(o_ref.dtype)
        rope = qf_ref[:, (N_HEADS + h) * LANES:(N_HEADS + h + 1) * LANES]
        o_ref[:, h * width + MLA_KV_RANK:(h + 1) * width] = _rope_group(
            rope, cos_ref[...], sin_ref[...]).astype(o_ref.dtype)


def _mla_q(qf, wuk_t, cos, sin, out_dtype):
    m = qf.shape[0]
    tm = _tile(m, 640, 16)
    width = N_HEADS * (MLA_KV_RANK + LANES)
    return pl.pallas_call(
        _mla_q_kernel, grid=(m // tm,),
        in_specs=[pl.BlockSpec((tm, qf.shape[1]), lambda i: (i, 0)),
                  pl.BlockSpec(wuk_t.shape, lambda i: (0, 0, 0)),
                  pl.BlockSpec((tm, LANES), lambda i: (i, 0)), pl.BlockSpec((tm, LANES), lambda i: (i, 0))],
        out_specs=pl.BlockSpec((tm, width), lambda i: (i, 0)),
        out_shape=jax.ShapeDtypeStruct((m, width), out_dtype),
        compiler_params=_cp(("parallel",)), name="mla_q")(qf, wuk_t, cos, sin)


def _mla_uv_kernel(o_ref, w_ref, d_ref):
    for h in range(N_HEADS):
        d_ref[:, h * MLA_VDH:(h + 1) * MLA_VDH] = jnp.dot(
            o_ref[:, h * MLA_KV_RANK:(h + 1) * MLA_KV_RANK], w_ref[h], preferred_element_type=F32).astype(BF16)


def _mla_uv(o_lat, wuv):
    m = o_lat.shape[0]
    tm = _tile(m, 640, 16)
    return pl.pallas_call(
        _mla_uv_kernel, grid=(m // tm,),
        in_specs=[pl.BlockSpec((tm, o_lat.shape[1]), lambda i: (i, 0)), pl.BlockSpec(wuv.shape, lambda i: (0, 0, 0))],
        out_specs=pl.BlockSpec((tm, N_HEADS * MLA_VDH), lambda i: (i, 0)),
        out_shape=jax.ShapeDtypeStruct((m, N_HEADS * MLA_VDH), BF16),
        compiler_params=_cp(("parallel",)), name="mla_uv")(o_lat, wuv)


def _prep_w_in_even(w):
    d = w.shape[0]
    pad = jnp.zeros((d, 3200 - 3144), w.dtype)
    cols = [w[:, 0:1024], w[:, 1280:2304], w[:, 2560:3072], w[:, 1024:1280], w[:, 2304:2560], w[:, 3072:3144], pad]
    return jnp.concatenate(cols, axis=1).astype(BF16)


def _prep_w_in_odd(w):
    d = w.shape[0]
    mq = jnp.pad(w[:, 0:512].reshape(d, N_HEADS, MOBA_DH), ((0, 0), (0, 0), (0, LANES - MOBA_DH))).reshape(d, -1)
    kr = jnp.pad(w[:, 1408:1472], ((0, 0), (0, LANES - MLA_ROPE)))
    return jnp.concatenate([mq, w[:, 640:1152], w[:, 1152:1408], w[:, 512:640], kr], axis=1).astype(BF16)


def _prep_w_q_up(w):
    r = w.shape[0]
    w3 = w.reshape(r, N_HEADS, MLA_NOPE + MLA_ROPE)
    nope = w3[:, :, :MLA_NOPE].reshape(r, -1)
    rope = jnp.pad(w3[:, :, MLA_NOPE:], ((0, 0), (0, 0), (0, LANES - MLA_ROPE))).reshape(r, -1)
    return jnp.concatenate([nope, rope], axis=1).astype(BF16)


def _rope_tables(pos):
    half = MLA_ROPE // 2
    freq = jnp.power(ROPE_THETA, -jnp.arange(half, dtype=F32) / half)
    ang = pos.astype(F32)[:, None] * freq
    cos, sin = jnp.cos(ang), jnp.sin(ang)
    zero = jnp.zeros((pos.shape[0], LANES - MLA_ROPE), F32)
    return jnp.concatenate([cos, cos, zero], axis=1), jnp.concatenate([-sin, sin, zero], axis=1)


def _sample_rows(x, n_rows=SROWS):
    return jnp.pad(x, ((0, 0), (0, n_rows - x.shape[1]), (0, 0)))


def kernel(x_prompt, x_sample, cache_diff_kv, cache_dsa_kv, cache_dsa_idx, cache_moba_kv, cache_mla, page_table, p_prompt, p_sample, rel_bias, w_in_even, diff_lambda, diff_subln_g, w_out_even, ffn_w_in, ffn_w_out, w_in_odd, mla_q_norm_g, mla_kv_norm_g, mla_w_q_up, mla_w_uk, mla_w_uv, w_out_odd, moe_router, moe_w_in, moe_w_out, ln1_g, ln1_b, ln2_g, ln2_b, ple_w_gate, ple_w_proj):
    b, l, d = x_prompt.shape
    ns, dec_seq, _ = x_sample.shape
    depth = ln1_g.shape[0]
    n_pages = page_table.shape[1]
    past = n_pages * PAGE
    n_prompt = b * l
    assert dec_seq == 1 and past % MOBA_BLOCK == 0 and n_pages % CHUNK_PAGES == 0
    assert l % TK == 0 and TK == MOBA_BLOCK and TK == 2 * TQ
    alpha = (2 * depth) ** 0.25

    x = jnp.concatenate([x_prompt.reshape(n_prompt, d), x_sample.reshape(ns, d)], axis=0)
    xb = x.astype(BF16)
    ple = jnp.concatenate([p_prompt.reshape(depth, n_prompt, -1), p_sample.reshape(depth, ns, -1)], axis=1).astype(BF16)

    tbl8 = _bias_tables_prompt(rel_bias)
    tbl0 = _bias_tables_prompt(jnp.zeros((REL_BUCKETS, 1), F32))
    tbs = _bias_table_sample(rel_bias, past)
    pos = jnp.concatenate([jnp.tile(jnp.arange(l), b), jnp.full((ns,), past)])
    cos, sin = _rope_tables(pos)
    lane = jnp.arange(LANES)
    idx_pool_t = jnp.swapaxes(cache_dsa_idx, 2, 3)
    mla_pool_t = jnp.swapaxes(cache_mla, 2, 3)
    n_tiles = (2 * (n_prompt + ns) + N_EXPERTS * (MOE_TILE - 1)) // MOE_TILE
    moe_w_out_b = moe_w_out.astype(BF16)

    rows_diff, rows_dsa, rows_idx, rows_moba, rows_mla = [], [], [], [], []
    for i in range(depth):
        j = i // 2
        if i % 2 == 0:
            lam_init = 0.8 - 0.6 * math.exp(-0.3 * i)
            h = _matmul(xb, _prep_w_in_even(w_in_even[j]), F32, "in_proj_even", tn_pref=640)
            r_diff, r_dsa, r_idx = h[:, 2560:2816], h[:, 2816:3072], h[:, 3072:3072 + IDX_DH]
            rows_diff.append(r_diff); rows_dsa.append(r_dsa); rows_idx.append(r_idx)
            hp = h[:n_prompt].reshape(b, l, -1)
            hs = h[n_prompt:]
            a_p = _flash_t("A", jnp.swapaxes(hp[:, :, 0:1024], 1, 2), hp, jnp.swapaxes(hp[:, :, 2688:2816], 1, 2),
                           tbl8, k_blk=20, k_width=LANES, out_width=1024, groups=2 * N_HEADS,
                           scale=DIFF_DH ** -0.5, lam=diff_lambda[j], subln_g=diff_subln_g[j], lam_init=lam_init,
                           name="diff_attn_prompt")
            topk = min(DSA_TOPK, l // 4)
            sel_t = _idx_prompt(jnp.swapaxes(hp[:, :, 2048:2560], 1, 2), hp,
                                jnp.swapaxes(hp[:, :, 3136:3136 + IDX_HEADS], 1, 2), topk)
            s_p = _flash_t("B", jnp.swapaxes(hp[:, :, 1024:2048], 1, 2), hp, jnp.swapaxes(hp[:, :, 2944:3072], 1, 2),
                           tbl8, k_blk=22, k_width=LANES, out_width=1024, groups=N_HEADS, scale=DSA_DH ** -0.5,
                           mask_t=sel_t, name="dsa_attn_prompt")
            dq = hs[:, 0:1024].reshape(ns, N_HEADS, LANES)
            q_a = jnp.concatenate([jnp.where(lane < DIFF_DH, dq, 0.0), jnp.where(lane >= DIFF_DH, dq, 0.0)], axis=1)
            o_a = _sample_attn("A", q_a, cache_diff_kv, j, page_table, hs[:, None, 2560:2816], tbs,
                               pieces=((0, LANES),), v_lo=LANES, v_hi=2 * LANES, scale=DIFF_DH ** -0.5,
                               name="diff_attn_sample")
            a_s = _diff_post(o_a, diff_lambda[j], diff_subln_g[j], lam_init).reshape(ns, -1)
            iq = _sample_rows(hs[:, 2048:2560].reshape(ns, IDX_HEADS, IDX_DH))
            iw = _sample_rows(jnp.broadcast_to(hs[:, 3072 + IDX_DH:3072 + IDX_DH + IDX_HEADS, None],
                                               (ns, IDX_HEADS, LANES)))
            topk_s = min(DSA_TOPK, (past + 1) // 4)
            sel_s = _idx_sample(iq, iw, idx_pool_t, j, page_table, hs[:, None, 3072:3072 + IDX_DH], topk_s)
            q_s = _sample_rows(hs[:, 1024:2048].reshape(ns, N_HEADS, LANES))
            o_s = _sample_attn("B", q_s, cache_dsa_kv, j, page_table, hs[:, None, 2816:3072], tbs,
                               pieces=((0, LANES),), v_lo=LANES, v_hi=2 * LANES, scale=DSA_DH ** -0.5,
                               extra=sel_s, name="dsa_attn_sample")
            s_s = o_s[:, :N_HEADS].reshape(ns, -1)
            mix_p = jnp.concatenate([a_p.reshape(n_prompt, -1), s_p.reshape(n_prompt, -1)], axis=1)
            mix_s = jnp.concatenate([a_s, s_s], axis=1).astype(BF16)
            mix = jnp.concatenate([mix_p, mix_s], axis=0)
            hout = _matmul(mix, w_out_even[j].astype(BF16), F32, "out_proj_even")
        else:
            h = _matmul(xb, _prep_w_in_odd(w_in_odd[j]), F32, "in_proj_odd")
            r_moba = h[:, 1792:1920]
            rows_moba.append(r_moba)
            qn, kvrow = _mla_prep(h, mla_q_norm_g[j], mla_kv_norm_g[j], cos, sin)
            rows_mla.append(kvrow[:, :MLA_KV_RANK + MLA_ROPE])
            qf = _matmul(qn, _prep_w_q_up(mla_w_q_up[j]), F32, "mla_q_up")
            wuk_t = jnp.transpose(mla_w_uk[j], (1, 2, 0)).astype(BF16)
            wuv = jnp.transpose(mla_w_uv[j], (1, 0, 2)).astype(BF16)
            qmla = _mla_q(qf, wuk_t, cos, sin, F32)
            hp = h[:n_prompt].reshape(b, l, -1)
            hs = h[n_prompt:]
            m_p = _flash_t("C", jnp.swapaxes(hp[:, :, 0:1024], 1, 2), hp, jnp.swapaxes(hp[:, :, 1792:1920], 1, 2),
                           tbl8, k_blk=14, k_width=LANES, out_width=N_HEADS * MOBA_DH, groups=N_HEADS,
                           scale=MOBA_DH ** -0.5, n_sel=min(MOBA_TOPK, l // MOBA_BLOCK), name="moba_attn_prompt")
            width = MLA_KV_RANK + LANES
            kv_p = kvrow[:n_prompt].reshape(b, l, -1)
            o_lat_p = _flash_t("D", jnp.swapaxes(qmla[:n_prompt].reshape(b, l, -1), 1, 2), kv_p,
                               jnp.swapaxes(kv_p[:, :, :MLA_KV_RANK], 1, 2), tbl0, k_blk=0, k_width=width,
                               out_width=N_HEADS * MLA_KV_RANK, groups=N_HEADS,
                               scale=(MLA_NOPE + MLA_ROPE) ** -0.5, name="mla_attn_prompt")
            q_c = _sample_rows(hs[:, 0:1024].reshape(ns, N_HEADS, LANES))
            o_c = _sample_attn("C", q_c, cache_moba_kv, j, page_table, hs[:, None, 1792:1920], tbs,
                               pieces=((0, LANES),), v_lo=0, v_hi=LANES, scale=MOBA_DH ** -0.5,
                               n_sel=min(MOBA_TOPK, (past + 1) // MOBA_BLOCK), name="moba_attn_sample")
            m_s = o_c[:, :N_HEADS, MOBA_DH:].reshape(ns, -1)
            q_d = _sample_rows(qmla[n_prompt:].reshape(ns, N_HEADS, width)[:, :, :MLA_KV_RANK + MLA_ROPE])
            o_d = _sample_attn("D", q_d, mla_pool_t, j, page_table, kvrow[n_prompt:, None, :MLA_KV_RANK + MLA_ROPE],
                               None, pieces=((0, MLA_KV_RANK), (MLA_KV_RANK, MLA_KV_RANK + MLA_ROPE)),
                               v_lo=0, v_hi=MLA_KV_RANK, scale=(MLA_NOPE + MLA_ROPE) ** -0.5, transposed=True,
                               name="mla_attn_sample")
            o_lat = jnp.concatenate([o_lat_p.reshape(n_prompt, -1),
                                     o_d[:, :N_HEADS].reshape(ns, -1).astype(BF16)], axis=0)
            dmla = _mla_uv(o_lat, wuv)
            m_all = jnp.concatenate([m_p.reshape(n_prompt, -1), m_s.astype(BF16)], axis=0)
            mix = jnp.concatenate([m_all, dmla], axis=1)
            hout = _matmul(mix, w_out_odd[j].astype(BF16), F32, "out_proj_odd")

        x, xb = _deepnorm_ln(x, hout, ln1_g[i], ln1_b[i], alpha, "deepnorm_ln1")
        if i % 2 == 0:
            mid = _swiglu(xb, ffn_w_in[j][None].astype(BF16), "ffn_in")
            f = _matmul(mid, ffn_w_out[j].astype(BF16), F32, "ffn_out")
            x, xb = _deepnorm_ln(x, f, ln2_g[i], ln2_b[i], alpha, "deepnorm_ln2")
        else:
            router = jnp.pad(moe_router[j], ((0, 0), (0, LANES - N_EXPERTS))).astype(BF16)
            route = _router(xb, router, "moe_router")
            slot1, slot2, token_of_slot, tile_expert, n_used = _dispatch_plan(route, n_tiles)
            xs = _gather_rows(x, token_of_slot)
            mid = _grouped_swiglu(xs, moe_w_in, j, tile_expert, n_used)
            ys = _grouped_matmul(mid, moe_w_out_b, j, tile_expert, n_used)
            x, xb = _moe_combine_ln(ys, slot1, slot2, route, x, ln2_g[i], ln2_b[i], alpha)
        x, xb = _ple(xb, x, ple_w_gate[i].astype(BF16), ple[i], ple_w_proj[i].astype(BF16), "ple_gate")

    def split(rows, width):
        r = jnp.stack(rows, axis=0)
        rp = jnp.transpose(r[:, :n_prompt].reshape(-1, b, l, width), (1, 0, 2, 3))
        rs = jnp.transpose(r[:, n_prompt:].reshape(-1, ns, 1, width), (1, 0, 2, 3))
        return rp, rs

    dp, ds = split(rows_diff, 256)
    sp, ss = split(rows_dsa, 256)
    ip, is_ = split(rows_idx, IDX_DH)
    mp, ms = split(rows_moba, 2 * MOBA_DH)
    lp, ls = split(rows_mla, MLA_KV_RANK + MLA_ROPE)
    return (x[:n_prompt].reshape(b, l, d), x[n_prompt:].reshape(ns, 1, d), dp, ds, sp, ss, ip, is_, mp, ms, lp, ls)
```

```python
import functools
import math

import jax
import jax.numpy as jnp
from jax import lax
from jax.experimental import pallas as pl
from jax.experimental.pallas import tpu as pltpu

F32, BF16, I32 = jnp.float32, jnp.bfloat16, jnp.int32

N_HEADS = 8
DIFF_DH = 64
DSA_DH = 128
IDX_HEADS = 8
IDX_DH = 64
DSA_TOPK = 256
MOBA_DH = 64
MOBA_BLOCK = 256
MOBA_TOPK = 3
MLA_Q_RANK = 512
MLA_KV_RANK = 256
MLA_NOPE = 128
MLA_ROPE = 64
MLA_VDH = 128
ROPE_THETA = 10000.0
REL_BUCKETS = 32
REL_MAX_DIST = 128
N_EXPERTS = 8
PAGE = 128
NEG = -1e30
LN_EPS = 1e-5
INT_MIN = -(2 ** 31)

LANES = 128
TQ = 128
TK = 256
SROWS = 16
CHUNK_PAGES = 16
VMEM_LIMIT = 52 * 1024 * 1024


def _cp(sem, vmem=VMEM_LIMIT):
    return pltpu.CompilerParams(dimension_semantics=sem, vmem_limit_bytes=vmem)


def _tile(n, pref, mult):
    best = None
    for t in range(mult, min(n, pref) + 1, mult):
        if n % t == 0:
            best = t
    return best if best is not None else n


def _mm_kernel(x_ref, w_ref, o_ref):
    o_ref[...] = jnp.dot(x_ref[...], w_ref[...], preferred_element_type=F32).astype(o_ref.dtype)


def _matmul(x, w, out_dtype, name, tn_pref=512):
    m, k = x.shape
    n = w.shape[1]
    tm = _tile(m, 640, 16)
    tn = _tile(n, tn_pref, LANES)
    return pl.pallas_call(
        _mm_kernel, grid=(m // tm, n // tn),
        in_specs=[pl.BlockSpec((tm, k), lambda i, j: (i, 0)), pl.BlockSpec((k, tn), lambda i, j: (0, j))],
        out_specs=pl.BlockSpec((tm, tn), lambda i, j: (i, j)),
        out_shape=jax.ShapeDtypeStruct((m, n), out_dtype),
        compiler_params=_cp(("parallel", "arbitrary")), name=name)(x, w)


def _swiglu_kernel(x_ref, wg_ref, wu_ref, o_ref):
    x = x_ref[...]
    g = jnp.dot(x, wg_ref[...], preferred_element_type=F32)
    u = jnp.dot(x, wu_ref[...], preferred_element_type=F32)
    o_ref[...] = (g * jax.nn.sigmoid(g) * u).astype(o_ref.dtype)


def _swiglu(x, w_in, name):
    m, k = x.shape
    e, _, f2 = w_in.shape
    f = f2 // 2
    tm = _tile(m, 640, 16)
    tn = _tile(f, 512, LANES)
    nper = f // tn
    return pl.pallas_call(
        _swiglu_kernel, grid=(m // tm, e * nper),
        in_specs=[pl.BlockSpec((tm, k), lambda i, j: (i, 0)),
                  pl.BlockSpec((None, k, tn), lambda i, j: (j // nper, 0, j % nper)),
                  pl.BlockSpec((None, k, tn), lambda i, j: (j // nper, 0, j % nper + nper))],
        out_specs=pl.BlockSpec((tm, tn), lambda i, j: (i, j)),
        out_shape=jax.ShapeDtypeStruct((m, e * f), BF16),
        compiler_params=_cp(("parallel", "arbitrary")), name=name)(x, w_in, w_in)


def _ln_kernel(x_ref, h_ref, g_ref, b_ref, o_ref, ob_ref, *, alpha):
    z = alpha * x_ref[...] + h_ref[...]
    mu = jnp.mean(z, axis=-1, keepdims=True)
    zc = z - mu
    var = jnp.mean(zc * zc, axis=-1, keepdims=True)
    y = zc * lax.rsqrt(var + LN_EPS) * g_ref[...] + b_ref[...]
    o_ref[...] = y
    ob_ref[...] = y.astype(BF16)


def _deepnorm_ln(x, h, g, b, alpha, name):
    m, d = x.shape
    tm = _tile(m, 320, 16)
    row = pl.BlockSpec((tm, d), lambda i: (i, 0))
    vec = pl.BlockSpec((1, d), lambda i: (0, 0))
    return pl.pallas_call(
        functools.partial(_ln_kernel, alpha=alpha), grid=(m // tm,),
        in_specs=[row, row, vec, vec], out_specs=[row, row],
        out_shape=[jax.ShapeDtypeStruct((m, d), F32), jax.ShapeDtypeStruct((m, d), BF16)],
        compiler_params=_cp(("parallel",)), name=name)(x, h, g.reshape(1, d), b.reshape(1, d))


def _ple_kernel(xb_ref, x_ref, wg_ref, pb_ref, wp_ref, o_ref, ob_ref):
    gate = jax.nn.sigmoid(jnp.dot(xb_ref[...], wg_ref[...], preferred_element_type=F32))
    proj = jnp.dot(pb_ref[...], wp_ref[...], preferred_element_type=F32)
    y = x_ref[...] + gate * proj
    o_ref[...] = y
    ob_ref[...] = y.astype(BF16)


def _ple(xb, x, wg, pb, wp, name):
    m, d = x.shape
    pd = pb.shape[1]
    tm = _tile(m, 640, 16)
    tn = _tile(d, 512, LANES)
    return pl.pallas_call(
        _ple_kernel, grid=(m // tm, d // tn),
        in_specs=[pl.BlockSpec((tm, d), lambda i, j: (i, 0)), pl.BlockSpec((tm, tn), lambda i, j: (i, j)),
                  pl.BlockSpec((d, tn), lambda i, j: (0, j)), pl.BlockSpec((tm, pd), lambda i, j: (i, 0)),
                  pl.BlockSpec((pd, tn), lambda i, j: (0, j))],
        out_specs=[pl.BlockSpec((tm, tn), lambda i, j: (i, j)), pl.BlockSpec((tm, tn), lambda i, j: (i, j))],
        out_shape=[jax.ShapeDtypeStruct((m, d), F32), jax.ShapeDtypeStruct((m, d), BF16)],
        compiler_params=_cp(("parallel", "arbitrary")), name=name)(xb, x, wg, pb, wp)


def _router_kernel(xb_ref, w_ref, o_ref):
    logits = jnp.dot(xb_ref[...], w_ref[...], preferred_element_type=F32)
    lane = lax.broadcasted_iota(I32, logits.shape, 1)
    logits = jnp.where(lane < N_EXPERTS, logits, -jnp.inf)
    m1 = jnp.max(logits, axis=-1, keepdims=True)
    i1 = jnp.min(jnp.where(logits == m1, lane, LANES), axis=-1, keepdims=True)
    rest = jnp.where(lane == i1, -jnp.inf, logits)
    m2 = jnp.max(rest, axis=-1, keepdims=True)
    i2 = jnp.min(jnp.where(rest == m2, lane, LANES), axis=-1, keepdims=True)
    e2 = jnp.exp(m2 - m1)
    den = 1.0 + e2
    o_ref[...] = jnp.where(lane == 0, i1.astype(F32),
                           jnp.where(lane == 1, i2.astype(F32),
                                     jnp.where(lane == 2, 1.0 / den, jnp.where(lane == 3, e2 / den, 0.0))))


def _router(xb, w, name):
    m, d = xb.shape
    tm = _tile(m, 640, 16)
    return pl.pallas_call(
        _router_kernel, grid=(m // tm,),
        in_specs=[pl.BlockSpec((tm, d), lambda i: (i, 0)), pl.BlockSpec((d, LANES), lambda i: (0, 0))],
        out_specs=pl.BlockSpec((tm, LANES), lambda i: (i, 0)),
        out_shape=jax.ShapeDtypeStruct((m, LANES), F32),
        compiler_params=_cp(("parallel",)), name=name)(xb, w)


MOE_TILE = 512
GATHER_ROWS = 256


def _row_copy(src_hbm, src_row, dst, dst_row, sem):
    return pltpu.make_async_copy(src_hbm.at[pl.ds(src_row, 1)], dst.at[pl.ds(dst_row, 1)], sem)


def _gather_rows_kernel(tok_ref, x_hbm, o_ref, buf, sem):
    s = pl.program_id(0)
    slot = s % 2
    rows = buf.shape[1]

    def start(step, into):
        def issue(pair, c):
            for prio in range(2):
                r = 2 * pair + prio
                _row_copy(x_hbm, tok_ref[step * rows + r], buf.at[into], r, sem.at[into]).start(priority=prio)
            return c
        lax.fori_loop(0, rows // 2, issue, 0)

    @pl.when(s == 0)
    def _():
        start(0, 0)

    @pl.when(s + 1 < pl.num_programs(0))
    def _():
        start(s + 1, 1 - slot)

    def drain(r, c):
        _row_copy(x_hbm, 0, buf.at[slot], r, sem.at[slot]).wait()
        return c

    lax.fori_loop(0, rows, drain, 0)
    o_ref[...] = buf[slot].astype(o_ref.dtype)


def _gather_rows(x, token_of_slot):
    p = token_of_slot.shape[0]
    d = x.shape[1]
    return pl.pallas_call(
        _gather_rows_kernel,
        grid_spec=pltpu.PrefetchScalarGridSpec(
            num_scalar_prefetch=1, grid=(p // GATHER_ROWS,),
            in_specs=[pl.BlockSpec(memory_space=pl.ANY)],
            out_specs=pl.BlockSpec((GATHER_ROWS, d), lambda i, tok: (i, 0)),
            scratch_shapes=[pltpu.VMEM((2, GATHER_ROWS, d), x.dtype), pltpu.SemaphoreType.DMA((2,))]),
        out_shape=jax.ShapeDtypeStruct((p, d), BF16),
        compiler_params=_cp(("arbitrary",)), name="moe_gather")(token_of_slot, x)


def _gswiglu_kernel(te_ref, nu_ref, x_ref, wg_ref, wu_ref, o_ref, wg_scr, wu_scr):
    i = pl.program_id(1)

    @pl.when(jnp.logical_or(i == 0, te_ref[i] != te_ref[jnp.maximum(i - 1, 0)]))
    def _():
        wg_scr[...] = wg_ref[...].astype(BF16)
        wu_scr[...] = wu_ref[...].astype(BF16)

    @pl.when(i < nu_ref[0])
    def _():
        x = x_ref[...]
        g = jnp.dot(x, wg_scr[...], preferred_element_type=F32)
        u = jnp.dot(x, wu_scr[...], preferred_element_type=F32)
        o_ref[...] = (g * jax.nn.sigmoid(g) * u).astype(o_ref.dtype)

    @pl.when(i >= nu_ref[0])
    def _():
        o_ref[...] = jnp.zeros_like(o_ref)


def _grouped_swiglu(xs, w_in, layer, tile_expert, n_used):
    p, k = xs.shape
    f = w_in.shape[3] // 2
    tn = _tile(f, 512, LANES)
    nper = f // tn
    return pl.pallas_call(
        _gswiglu_kernel,
        grid_spec=pltpu.PrefetchScalarGridSpec(
            num_scalar_prefetch=2, grid=(nper, p // MOE_TILE),
            in_specs=[pl.BlockSpec((MOE_TILE, k), lambda j, i, te, nu: (i, 0)),
                      pl.BlockSpec((None, None, k, tn), lambda j, i, te, nu: (layer, te[i], 0, j)),
                      pl.BlockSpec((None, None, k, tn), lambda j, i, te, nu: (layer, te[i], 0, j + nper))],
            out_specs=pl.BlockSpec((MOE_TILE, tn), lambda j, i, te, nu: (i, j)),
            scratch_shapes=[pltpu.VMEM((k, tn), BF16), pltpu.VMEM((k, tn), BF16)]),
        out_shape=jax.ShapeDtypeStruct((p, f), BF16),
        compiler_params=_cp(("arbitrary", "arbitrary")), name="moe_in")(tile_expert, n_used, xs, w_in, w_in)


def _gmm_kernel(te_ref, nu_ref, h_ref, w_ref, o_ref):
    @pl.when(pl.program_id(1) < nu_ref[0])
    def _():
        o_ref[...] = jnp.dot(h_ref[...], w_ref[...], preferred_element_type=F32)

    @pl.when(pl.program_id(1) >= nu_ref[0])
    def _():
        o_ref[...] = jnp.zeros_like(o_ref)


def _grouped_matmul(h, w, layer, tile_expert, n_used):
    p, f = h.shape
    n = w.shape[3]
    tn = _tile(n, 512, LANES)
    return pl.pallas_call(
        _gmm_kernel,
        grid_spec=pltpu.PrefetchScalarGridSpec(
            num_scalar_prefetch=2, grid=(n // tn, p // MOE_TILE),
            in_specs=[pl.BlockSpec((MOE_TILE, f), lambda j, i, te, nu: (i, 0)),
                      pl.BlockSpec((None, None, f, tn), lambda j, i, te, nu: (layer, te[i], 0, j))],
            out_specs=pl.BlockSpec((MOE_TILE, tn), lambda j, i, te, nu: (i, j))),
        out_shape=jax.ShapeDtypeStruct((p, n), F32),
        compiler_params=_cp(("arbitrary", "arbitrary")), name="moe_out")(tile_expert, n_used, h, w)


def _moe_combine_ln_kernel(s1_ref, s2_ref, route_ref, x_ref, ys_hbm, g_ref, b_ref, o_ref, ob_ref,
                           buf1, buf2, sem, *, alpha):
    tm = buf1.shape[0]
    base = pl.program_id(0) * tm

    def issue(r, c):
        _row_copy(ys_hbm, s1_ref[base + r], buf1, r, sem).start(priority=0)
        _row_copy(ys_hbm, s2_ref[base + r], buf2, r, sem).start(priority=1)
        return c

    def drain(r, c):
        _row_copy(ys_hbm, 0, buf1, r, sem).wait()
        _row_copy(ys_hbm, 0, buf2, r, sem).wait()
        return c

    lax.fori_loop(0, tm, issue, 0)
    lax.fori_loop(0, tm, drain, 0)
    f = route_ref[:, 2:3] * buf1[...] + route_ref[:, 3:4] * buf2[...]
    z = alpha * x_ref[...] + f
    mu = jnp.mean(z, axis=-1, keepdims=True)
    zc = z - mu
    var = jnp.mean(zc * zc, axis=-1, keepdims=True)
    y = zc * lax.rsqrt(var + LN_EPS) * g_ref[...] + b_ref[...]
    o_ref[...] = y
    ob_ref[...] = y.astype(BF16)


def _moe_combine_ln(ys, slot1, slot2, route, x, g, b, alpha):
    m, d = x.shape
    tm = _tile(m, 320, 16)
    row = lambda i, s1, s2: (i, 0)
    vec = pl.BlockSpec((1, d), lambda i, s1, s2: (0, 0))
    return pl.pallas_call(
        functools.partial(_moe_combine_ln_kernel, alpha=alpha),
        grid_spec=pltpu.PrefetchScalarGridSpec(
            num_scalar_prefetch=2, grid=(m // tm,),
            in_specs=[pl.BlockSpec((tm, LANES), row), pl.BlockSpec((tm, d), row),
                      pl.BlockSpec(memory_space=pl.ANY), vec, vec],
            out_specs=[pl.BlockSpec((tm, d), row), pl.BlockSpec((tm, d), row)],
            scratch_shapes=[pltpu.VMEM((tm, d), F32), pltpu.VMEM((tm, d), F32), pltpu.SemaphoreType.DMA(())]),
        out_shape=[jax.ShapeDtypeStruct((m, d), F32), jax.ShapeDtypeStruct((m, d), BF16)],
        compiler_params=_cp(("arbitrary",)), name="moe_combine_ln")(
            slot1, slot2, route, x, ys, g.reshape(1, d), b.reshape(1, d))


def _dispatch_plan(route, n_tiles):
    m = route.shape[0]
    e = jnp.concatenate([route[:, 0], route[:, 1]]).astype(I32)
    onehot = (e[:, None] == jnp.arange(N_EXPERTS)[None, :]).astype(I32)
    pos = jnp.sum((jnp.cumsum(onehot, axis=0) - 1) * onehot, axis=1)
    counts = jnp.sum(onehot, axis=0)
    padded = (counts + MOE_TILE - 1) // MOE_TILE * MOE_TILE
    ends = jnp.cumsum(padded)
    slot = (ends - padded)[e] + pos
    token = jnp.tile(jnp.arange(m, dtype=I32), 2)
    token_of_slot = jnp.zeros((n_tiles * MOE_TILE,), I32).at[slot].set(token)
    n_used = (ends[-1] // MOE_TILE).astype(I32)
    tile_start = jnp.arange(n_tiles, dtype=I32) * MOE_TILE
    tile_expert = jnp.sum((tile_start[:, None] >= ends[None, :]).astype(I32), axis=1)
    last = jnp.sum((ends[-1] - 1 >= ends).astype(I32))
    tile_expert = jnp.minimum(tile_expert, last).astype(I32)
    return slot[:m].astype(I32), slot[m:].astype(I32), token_of_slot, tile_expert, n_used.reshape(1)


def _t5_bucket(d):
    n = jnp.maximum(d, 0)
    max_exact = REL_BUCKETS // 2
    nf = jnp.maximum(n, 1).astype(F32)
    large = max_exact + (jnp.log(nf / max_exact) / math.log(REL_MAX_DIST / max_exact)
                         * (REL_BUCKETS - max_exact)).astype(I32)
    large = jnp.minimum(large, REL_BUCKETS - 1)
    return jnp.where(n < max_exact, n, large)


def _bias_rows(bucket, rb_ref, h):
    acc = jnp.zeros(bucket.shape, F32)
    for b in range(REL_BUCKETS):
        acc = jnp.where(bucket == b, rb_ref[b, h], acc)
    return acc


def _tbl_prompt_kernel(rb_ref, o_ref, *, hb):
    t = pl.program_id(0)
    off = (t // 2) * TK + (t % 2) * TQ
    k = lax.broadcasted_iota(I32, (TK, TQ), 0)
    q = lax.broadcasted_iota(I32, (TK, TQ), 1)
    d = off + q - k
    bucket = _t5_bucket(d)
    for h in range(hb):
        o_ref[0, :, h * TQ:(h + 1) * TQ] = jnp.where(d >= 0, _bias_rows(bucket, rb_ref, h), NEG)


def _bias_tables_prompt(rb):
    hb = rb.shape[1]
    return pl.pallas_call(
        functools.partial(_tbl_prompt_kernel, hb=hb), grid=(6,),
        in_specs=[pl.BlockSpec(memory_space=pltpu.SMEM)],
        out_specs=pl.BlockSpec((1, TK, hb * TQ), lambda t: (t, 0, 0)),
        out_shape=jax.ShapeDtypeStruct((6, TK, hb * TQ), F32),
        compiler_params=_cp(("arbitrary",)), name="bias_tables_prompt")(rb)


def _tbl_sample_kernel(rb_ref, o_ref, *, past):
    k = lax.broadcasted_iota(I32, (1, o_ref.shape[1]), 1)
    bucket = _t5_bucket(past - k)
    for h in range(N_HEADS):
        row = _bias_rows(bucket, rb_ref, h)
        o_ref[h:h + 1, :] = row
        o_ref[N_HEADS + h:N_HEADS + h + 1, :] = row


def _bias_table_sample(rb, past):
    w = past + LANES
    return pl.pallas_call(
        functools.partial(_tbl_sample_kernel, past=past),
        in_specs=[pl.BlockSpec(memory_space=pltpu.SMEM)],
        out_specs=pl.BlockSpec(memory_space=pltpu.VMEM),
        out_shape=jax.ShapeDtypeStruct((SROWS, w), F32),
        compiler_params=_cp(None), name="bias_table_sample")(rb)


def _topk_additive(score, key_ref, k, axis=1):
    n = score.shape[axis]
    stat_shape = (score.shape[0], 1) if axis == 1 else (1, score.shape[1])
    score = jnp.where(score == 0.0, 0.0, score)
    bits = pltpu.bitcast(score, I32)
    key_ref[...] = jnp.where(bits < 0, bits ^ 0x7FFFFFFF, bits)
    kf = float(k)

    slabs = 8 if (axis == 0 and n % (8 * 8) == 0) else 1

    def count(cond):
        ones = jnp.where(cond, 1.0, 0.0)
        if slabs == 1:
            return jnp.sum(ones, axis=axis, keepdims=True)
        step = n // slabs
        parts = [jnp.sum(ones[s * step:(s + 1) * step], axis=0, keepdims=True) for s in range(slabs)]
        while len(parts) > 1:
            parts = [parts[a] + parts[a + 1] for a in range(0, len(parts), 2)]
        return parts[0]

    t0 = jnp.where(count(key_ref[...] >= 0) >= kf, 0, INT_MIN).astype(I32)

    def value_bit(it, t):
        cand = t + lax.shift_left(jnp.int32(1), 30 - it)
        return jnp.where(count(key_ref[...] >= cand) >= kf, cand, t)

    t = lax.fori_loop(0, 31, value_bit, t0)
    need = kf - count(key_ref[...] > t)
    tied = count(key_ref[...] == t) > need
    idx_bits = max(1, (n - 1).bit_length())
    trips = (jnp.max(jnp.where(tied, 1.0, 0.0)) > 0.0).astype(I32) * idx_bits

    def index_bit(it, x):
        cand = x + lax.shift_left(jnp.int32(1), idx_bits - 1 - it)
        idx = lax.broadcasted_iota(I32, score.shape, axis)
        below = count(jnp.logical_and(key_ref[...] == t, idx < cand))
        return jnp.where(below < need, cand, x)

    x = lax.fori_loop(0, trips, index_bit, jnp.zeros(stat_shape, I32))
    key = key_ref[...]
    idx = lax.broadcasted_iota(I32, score.shape, axis)
    keep_eq = jnp.logical_or(jnp.logical_not(tied), idx <= x)
    sel = jnp.logical_or(key > t, jnp.logical_and(key == t, keep_eq))
    return jnp.where(sel, 0.0, NEG)


def _idx_prompt_kernel(iqt_ref, kx_ref, wt_ref, o_ref, key_ref, *, topk):
    i = pl.program_id(1)
    lk = kx_ref.shape[1]
    ik = kx_ref[0][:, 0:IDX_DH].astype(BF16)
    score = jnp.zeros((lk, TQ), F32)
    for h in range(IDX_HEADS):
        qh = iqt_ref[0, h * IDX_DH:(h + 1) * IDX_DH, :].astype(BF16)
        sc = jnp.maximum(jnp.dot(ik, qh, preferred_element_type=F32) * IDX_DH ** -0.5, 0.0)
        score = score + sc * (wt_ref[0, h:h + 1, :] * IDX_HEADS ** -0.5)
    kpos = lax.broadcasted_iota(I32, (lk, TQ), 0)
    qpos = i * TQ + lax.broadcasted_iota(I32, (lk, TQ), 1)
    score = jnp.where(kpos <= qpos, score, NEG)
    o_ref[0] = _topk_additive(score, key_ref, topk, axis=0).astype(BF16)


def _idx_prompt(iq_t, h3, iw_t, topk):
    b, l, _ = h3.shape
    return pl.pallas_call(
        functools.partial(_idx_prompt_kernel, topk=topk), grid=(b, l // TQ),
        in_specs=[pl.BlockSpec((1, IDX_HEADS * IDX_DH, TQ), lambda bi, i: (bi, 0, i)),
                  pl.BlockSpec((1, l, LANES), lambda bi, i: (bi, 0, 24)),
                  pl.BlockSpec((1, IDX_HEADS, TQ), lambda bi, i: (bi, 0, i))],
        out_specs=pl.BlockSpec((1, l, TQ), lambda bi, i: (bi, 0, i)),
        out_shape=jax.ShapeDtypeStruct((b, l, l), BF16),
        scratch_shapes=[pltpu.VMEM((l, TQ), I32)],
        compiler_params=_cp(("parallel", "arbitrary")), name="dsa_index_prompt")(iq_t, h3, iw_t)


def _flash_t_kernel(*refs, variant, scale, n_sel, lam_init):
    if variant == "A":
        qt_ref, k_ref, vt_ref, tbl_ref, lam_ref, g_ref, o_ref, qt_scr, s_scr, p_scr, m_scr, l_scr, a_scr, acc_scr = refs
    elif variant == "B":
        qt_ref, k_ref, vt_ref, tbl_ref, mask_ref, o_ref, qt_scr, s_scr, p_scr, m_scr, l_scr, a_scr, acc_scr = refs
    elif variant == "C":
        (qt_ref, k_ref, vt_ref, tbl_ref, o_ref, qt_scr, s_scr, p_scr, m_scr, l_scr, a_scr, acc_scr,
         sel_scr, mean_scr) = refs
    else:
        qt_ref, k_ref, vt_ref, tbl_ref, o_ref, qt_scr, s_scr, p_scr, m_scr, l_scr, a_scr, acc_scr = refs
    i = pl.program_id(1)
    dk, cols_total = qt_scr.shape
    groups = cols_total // TQ
    hb = tbl_ref.shape[2] // TQ
    own = (i * TQ) // TK
    parity = ((i * TQ) % TK) // TQ

    if variant == "A":
        feat = lax.broadcasted_iota(I32, (LANES, TQ), 0)
        for c in range(2):
            for h in range(N_HEADS):
                slab = qt_ref[0, h * LANES:(h + 1) * LANES, :]
                keep = (feat < DIFF_DH) if c == 0 else (feat >= DIFF_DH)
                g = c * N_HEADS + h
                qt_scr[:, g * TQ:(g + 1) * TQ] = jnp.where(keep, slab, 0.0).astype(BF16)
    else:
        for h in range(N_HEADS):
            qt_scr[:, h * TQ:(h + 1) * TQ] = qt_ref[0, h * dk:(h + 1) * dk, :].astype(BF16)
    m_scr[...] = jnp.full(m_scr.shape, NEG, F32)
    l_scr[...] = jnp.zeros(l_scr.shape, F32)
    acc_scr[...] = jnp.zeros(acc_scr.shape, F32)

    if variant == "C":
        nblk = k_ref.shape[1] // TK
        nb_pad = mean_scr.shape[0]
        mean_scr[...] = jnp.zeros(mean_scr.shape, F32)
        for n in range(nblk):
            mean_scr[n:n + 1, :] = jnp.sum(k_ref[0, n * TK:(n + 1) * TK, :], axis=0, keepdims=True) * (1.0 / TK)
        nrow = lax.broadcasted_iota(I32, (nb_pad, TQ), 0)
        for h in range(N_HEADS):
            gate = jnp.dot(mean_scr[...], qt_ref[0, h * LANES:(h + 1) * LANES, :],
                           preferred_element_type=F32)
            rank = jnp.zeros((nb_pad, TQ), F32)
            for n2 in range(nblk):
                other = gate[n2:n2 + 1, :]
                beats = jnp.where(other > gate, 1.0,
                                  jnp.where(other == gate, jnp.where(n2 < nrow, 1.0, 0.0), 0.0))
                rank = rank + beats * (n2 < own).astype(F32)
            add = jnp.where(nrow < own, jnp.where(rank < n_sel, 0.0, NEG), 0.0)
            for n in range(nblk):
                sel_scr[n, :, h * TQ:(h + 1) * TQ] = add[n:n + 1, :]

    def step(j, carry):
        off = pl.multiple_of(j * TK, TK)
        kb = k_ref[0, pl.ds(off, TK), :].astype(BF16)
        s_scr[...] = jnp.dot(kb, qt_scr[...], preferred_element_type=F32)
        t = jnp.minimum(own - j, 2) * 2 + parity
        for g in range(groups):
            cols = slice(g * TQ, (g + 1) * TQ)
            hcol = (g % hb) * TQ
            s = s_scr[:, cols] * scale + tbl_ref[t, :, hcol:hcol + TQ]
            if variant == "B":
                s = s + mask_ref[0, pl.ds(off, TK), :].astype(F32)
            if variant == "C":
                s = s + sel_scr[j, :, cols]
            m_prev = m_scr[:, cols]
            m_new = jnp.maximum(m_prev, jnp.max(s, axis=0, keepdims=True))
            alpha = jnp.exp(m_prev - m_new)
            p = jnp.exp(s - m_new)
            l_scr[:, cols] = alpha * l_scr[:, cols] + jnp.sum(p, axis=0, keepdims=True)
            m_scr[:, cols] = m_new
            a_scr[:, cols] = alpha
            p_scr[:, cols] = p.astype(BF16)
        vtb = vt_ref[0, :, pl.ds(off, TK)].astype(BF16)
        acc_scr[...] = acc_scr[...] * a_scr[...] + jnp.dot(vtb, p_scr[...], preferred_element_type=F32)
        return carry

    lax.fori_loop(0, own + 1, step, 0)

    def head_out(g):
        cols = slice(g * TQ, (g + 1) * TQ)
        return jnp.transpose(acc_scr[:, cols] / l_scr[:, cols])

    dv = acc_scr.shape[0]
    lane = lax.broadcasted_iota(I32, (TQ, LANES), 1)
    if variant == "A":
        lv = lam_ref[...]
        lam = (jnp.exp(jnp.sum(lv[0:1] * lv[1:2], axis=-1, keepdims=True))
               - jnp.exp(jnp.sum(lv[2:3] * lv[3:4], axis=-1, keepdims=True)) + lam_init)
        for h in range(N_HEADS):
            o = head_out(h) - lam * head_out(N_HEADS + h)
            o = o * lax.rsqrt(jnp.mean(o * o, axis=-1, keepdims=True) + LN_EPS)
            o_ref[0, :, h * LANES:(h + 1) * LANES] = (o * g_ref[...] * (1.0 - lam_init)).astype(o_ref.dtype)
    elif variant == "C":
        for pair in range(N_HEADS // 2):
            a0 = pltpu.roll(head_out(2 * pair), MOBA_DH, 1)
            a1 = head_out(2 * pair + 1)
            o_ref[0, :, pair * LANES:(pair + 1) * LANES] = jnp.where(lane < MOBA_DH, a0, a1).astype(o_ref.dtype)
    else:
        for h in range(N_HEADS):
            o_ref[0, :, h * dv:(h + 1) * dv] = head_out(h).astype(o_ref.dtype)


def _flash_t(variant, qt, k3, vt, tbl, *, k_blk, k_width, out_width, groups, scale,
             mask_t=None, lam=None, subln_g=None, lam_init=0.0, n_sel=0, name):
    b, wq, l = qt.shape
    dv = vt.shape[1]
    cols = groups * TQ
    in_specs = [pl.BlockSpec((1, wq, TQ), lambda bi, i: (bi, 0, i)),
                pl.BlockSpec((1, l, k_width), lambda bi, i: (bi, 0, k_blk)),
                pl.BlockSpec((1, dv, l), lambda bi, i: (bi, 0, 0)),
                pl.BlockSpec(tbl.shape, lambda bi, i: (0, 0, 0))]
    args = [qt, k3, vt, tbl]
    if variant == "A":
        in_specs += [pl.BlockSpec(lam.shape, lambda bi, i: (0, 0)), pl.BlockSpec((1, LANES), lambda bi, i: (0, 0))]
        args += [lam, subln_g.reshape(1, LANES)]
    if variant == "B":
        in_specs.append(pl.BlockSpec((1, l, TQ), lambda bi, i: (bi, 0, i)))
        args.append(mask_t)
    scratch = [pltpu.VMEM((k_width, cols), BF16), pltpu.VMEM((TK, cols), F32), pltpu.VMEM((TK, cols), BF16),
               pltpu.VMEM((1, cols), F32), pltpu.VMEM((1, cols), F32), pltpu.VMEM((1, cols), F32),
               pltpu.VMEM((dv, cols), F32)]
    if variant == "C":
        nb_pad = max(8, l // TK)
        scratch += [pltpu.VMEM((nb_pad, 1, cols), F32), pltpu.VMEM((nb_pad, LANES), F32)]
    kern = functools.partial(_flash_t_kernel, variant=variant, scale=scale, n_sel=n_sel, lam_init=lam_init)
    return pl.pallas_call(
        kern, grid=(b, l // TQ), in_specs=in_specs,
        out_specs=pl.BlockSpec((1, TQ, out_width), lambda bi, i: (bi, i, 0)),
        out_shape=jax.ShapeDtypeStruct((b, l, out_width), BF16),
        scratch_shapes=scratch, compiler_params=_cp(("parallel", "arbitrary")), name=name)(*args)


def _page_copy(pool_ref, pt_ref, buf, sem, layer, seq, slot, row, page):
    return pltpu.make_async_copy(pool_ref.at[pt_ref[seq, page], layer], buf.at[slot, row], sem.at[slot])


def _fetch_pages(pool_ref, pt_ref, buf, sem, layer, n_pages, seqs=1):
    s = pl.program_id(0)
    slot = s % 2

    def start(step, into):
        def one_seq(b, c):
            for p in range(n_pages):
                _page_copy(pool_ref, pt_ref, buf, sem, layer, step * seqs + b, into, b * n_pages + p, p).start()
            return c
        lax.fori_loop(0, seqs, one_seq, 0)

    @pl.when(s == 0)
    def _():
        start(0, 0)

    @pl.when(s + 1 < pl.num_programs(0))
    def _():
        start(s + 1, 1 - slot)

    def wait_seq(b, c):
        for p in range(n_pages):
            _page_copy(pool_ref, pt_ref, buf, sem, layer, 0, slot, b * n_pages + p, p).wait()
        return c
    lax.fori_loop(0, seqs, wait_seq, 0)
    return slot


def _chunk_t(buf, slot, c):
    pages = [buf[slot, c * CHUNK_PAGES + p] for p in range(CHUNK_PAGES)]
    return jnp.concatenate(pages, axis=1).astype(BF16)


def _sample_attn_kernel(*refs, variant, layer, pieces, v_lo, v_hi, scale, n_sel, transposed):
    if variant == "B":
        pt_ref, q_ref, pool_ref, new_ref, tb_ref, ex_ref, o_ref, buf, sem, s_scr, p_scr = refs
    elif variant == "C":
        pt_ref, q_ref, pool_ref, new_ref, tb_ref, o_ref, buf, sem, s_scr, p_scr, mean_scr, e_scr = refs
    else:
        pt_ref, q_ref, pool_ref, new_ref, tb_ref, o_ref, buf, sem, s_scr, p_scr = refs
    n_pages = buf.shape[1]
    past = n_pages * PAGE
    ckeys = CHUNK_PAGES * PAGE
    n_chunks = n_pages // CHUNK_PAGES
    slot = _fetch_pages(pool_ref, pt_ref, buf, sem, layer, n_pages)

    qf = q_ref[0]
    qb = qf.astype(BF16)

    if variant == "C":
        nblk = past // MOBA_BLOCK

        @pl.when(pl.program_id(0) == 0)
        def _():
            blk = lax.broadcasted_iota(I32, e_scr.shape, 0)
            key = lax.broadcasted_iota(I32, e_scr.shape, 1)
            e_scr[...] = jnp.where(key // MOBA_BLOCK == blk, 1.0, 0.0).astype(BF16)

    for c in range(n_chunks):
        s = None
        if transposed:
            kt = _chunk_t(buf, slot, c)
            for lo, hi in pieces:
                part = jnp.dot(qb[:, lo:hi], kt[lo:hi, :], preferred_element_type=F32)
                s = part if s is None else s + part
        else:
            kc = buf[slot, c * CHUNK_PAGES:(c + 1) * CHUNK_PAGES].reshape(ckeys, buf.shape[3])
            for lo, hi in pieces:
                part = lax.dot_general(qb[:, lo:hi], kc[:, lo:hi].astype(BF16), (((1,), (1,)), ((), ())),
                                       preferred_element_type=F32)
                s = part if s is None else s + part
        s_scr[:, c * ckeys:(c + 1) * ckeys] = s
        if variant == "C":
            per = ckeys // MOBA_BLOCK
            for n in range(per):
                mean_scr[c * per + n:c * per + n + 1, :] = (
                    jnp.sum(kc[n * MOBA_BLOCK:(n + 1) * MOBA_BLOCK, :], axis=0, keepdims=True) * (1.0 / MOBA_BLOCK))

    s = s_scr[...] * scale
    new = new_ref[0]
    s_new = None
    for lo, hi in pieces:
        part = jnp.sum(qf[:, lo:hi] * new[:, lo:hi], axis=-1, keepdims=True)
        s_new = part if s_new is None else s_new + part
    s_new = s_new * scale
    if tb_ref is not None:
        s = s + tb_ref[:, 0:past]
        s_new = s_new + tb_ref[:, past:past + 1]
    if variant == "B":
        s = s + ex_ref[0, :, 0:past]
        s_new = s_new + ex_ref[0, :, past:past + 1]
    if variant == "C":
        gate = lax.dot_general(qf, mean_scr[...], (((1,), (1,)), ((), ())), preferred_element_type=F32)
        nlane = lax.broadcasted_iota(I32, gate.shape, 1)
        rank = jnp.zeros(gate.shape, F32)
        for n2 in range(nblk):
            col = gate[:, n2:n2 + 1]
            rank = rank + jnp.where(col > gate, 1.0,
                                    jnp.where(col == gate, jnp.where(n2 < nlane, 1.0, 0.0), 0.0))
        chosen = jnp.where(rank < n_sel, 1.0, 0.0).astype(BF16)
        keep = jnp.dot(chosen, e_scr[...], preferred_element_type=F32)
        s = s + (keep - 1.0) * (-NEG)

    m = jnp.maximum(jnp.max(s, axis=-1, keepdims=True), s_new)
    p = jnp.exp(s - m)
    p_new = jnp.exp(s_new - m)
    denom = jnp.sum(p, axis=-1, keepdims=True) + p_new
    p_scr[...] = p.astype(BF16)
    acc = p_new * new[:, v_lo:v_hi]
    for c in range(n_chunks):
        pc = p_scr[:, c * ckeys:(c + 1) * ckeys]
        if transposed:
            vt = _chunk_t(buf, slot, c)[v_lo:v_hi, :]
            acc = acc + lax.dot_general(pc, vt, (((1,), (1,)), ((), ())), preferred_element_type=F32)
        else:
            vc = buf[slot, c * CHUNK_PAGES:(c + 1) * CHUNK_PAGES].reshape(ckeys, buf.shape[3])[:, v_lo:v_hi]
            acc = acc + jnp.dot(pc, vc.astype(BF16), preferred_element_type=F32)
    o_ref[0] = acc / denom


def _sample_attn(variant, q, pool, layer, page_table, new_rows, tb, *, pieces, v_lo, v_hi, scale,
                 extra=None, n_sel=0, transposed=False, name):
    ns, _, dq = q.shape
    n_pages = page_table.shape[1]
    past = n_pages * PAGE
    w = new_rows.shape[2]
    page_shape = (w, PAGE) if transposed else (PAGE, w)
    in_specs = [pl.BlockSpec((1, SROWS, dq), lambda s, pt: (s, 0, 0)),
                pl.BlockSpec(memory_space=pl.ANY),
                pl.BlockSpec((1, 1, w), lambda s, pt: (s, 0, 0))]
    args = [q, pool, new_rows]
    if tb is not None:
        in_specs.append(pl.BlockSpec(tb.shape, lambda s, pt: (0, 0)))
        args.append(tb)
    if variant == "B":
        in_specs.append(pl.BlockSpec((1, 1, past + LANES), lambda s, pt: (s, 0, 0)))
        args.append(extra)
    scratch = [pltpu.VMEM((2, n_pages) + page_shape, F32), pltpu.SemaphoreType.DMA((2,)),
               pltpu.VMEM((SROWS, past), F32), pltpu.VMEM((SROWS, past), BF16)]
    if variant == "C":
        nblk = past // MOBA_BLOCK
        scratch += [pltpu.VMEM((nblk, w), F32), pltpu.VMEM((nblk, past), BF16)]
    kern = functools.partial(_sample_attn_kernel, variant=variant, layer=layer, pieces=pieces,
                             v_lo=v_lo, v_hi=v_hi, scale=scale, n_sel=n_sel, transposed=transposed)
    if tb is None:
        inner = kern

        def kern(pt_ref, q_ref, pool_ref, new_ref, *rest):
            return inner(pt_ref, q_ref, pool_ref, new_ref, None, *rest)

    return pl.pallas_call(
        kern,
        grid_spec=pltpu.PrefetchScalarGridSpec(
            num_scalar_prefetch=1, grid=(ns,), in_specs=in_specs,
            out_specs=pl.BlockSpec((1, SROWS, v_hi - v_lo), lambda s, pt: (s, 0, 0)),
            scratch_shapes=scratch),
        out_shape=jax.ShapeDtypeStruct((ns, SROWS, v_hi - v_lo), F32),
        compiler_params=_cp(("arbitrary",)), name=name)(page_table, *args)


IDX_SEQS = 8


def _idx_sample_kernel(pt_ref, iq_ref, w_ref, pool_ref, new_ref, o_ref, buf, sem, row_scr, key_scr, *,
                       layer, topk, n_pages):
    past = n_pages * PAGE
    ckeys = CHUNK_PAGES * PAGE
    slot = _fetch_pages(pool_ref, pt_ref, buf, sem, layer, n_pages, seqs=IDX_SEQS)
    lane = lax.broadcasted_iota(I32, (1, LANES), 1)
    for b in range(IDX_SEQS):
        iqf = iq_ref[b]
        iqb = iqf.astype(BF16)
        wh = w_ref[b][:, 0:1] * IDX_HEADS ** -0.5
        for c in range(n_pages // CHUNK_PAGES):
            pages = [buf[slot, b * n_pages + c * CHUNK_PAGES + p] for p in range(CHUNK_PAGES)]
            kt = jnp.concatenate(pages, axis=1).astype(BF16)
            sc = jnp.maximum(jnp.dot(iqb, kt, preferred_element_type=F32) * IDX_DH ** -0.5, 0.0)
            row_scr[b:b + 1, c * ckeys:(c + 1) * ckeys] = jnp.sum(sc * wh, axis=0, keepdims=True)
        sc_new = jnp.maximum(jnp.sum(iqf * new_ref[b], axis=-1, keepdims=True) * IDX_DH ** -0.5, 0.0)
        score_new = jnp.sum(sc_new * wh, axis=0, keepdims=True)
        row_scr[b:b + 1, past:past + LANES] = jnp.where(lane == 0, score_new, -3e38)
    o_ref[0] = _topk_additive(row_scr[...], key_scr, topk)


def _idx_sample(iq, iw, pool_t, layer, page_table, new_rows, topk):
    ns = iq.shape[0]
    n_pages = page_table.shape[1]
    past = n_pages * PAGE
    wide = past + LANES
    out = pl.pallas_call(
        functools.partial(_idx_sample_kernel, layer=layer, topk=topk, n_pages=n_pages),
        grid_spec=pltpu.PrefetchScalarGridSpec(
            num_scalar_prefetch=1, grid=(ns // IDX_SEQS,),
            in_specs=[pl.BlockSpec((IDX_SEQS, SROWS, IDX_DH), lambda s, pt: (s, 0, 0)),
                      pl.BlockSpec((IDX_SEQS, SROWS, LANES), lambda s, pt: (s, 0, 0)),
                      pl.BlockSpec(memory_space=pl.ANY),
                      pl.BlockSpec((IDX_SEQS, 1, IDX_DH), lambda s, pt: (s, 0, 0))],
            out_specs=pl.BlockSpec((1, IDX_SEQS, wide), lambda s, pt: (s, 0, 0)),
            scratch_shapes=[pltpu.VMEM((2, IDX_SEQS * n_pages, IDX_DH, PAGE), F32), pltpu.SemaphoreType.DMA((2,)),
                            pltpu.VMEM((IDX_SEQS, wide), F32), pltpu.VMEM((IDX_SEQS, wide), I32)]),
        out_shape=jax.ShapeDtypeStruct((ns // IDX_SEQS, IDX_SEQS, wide), F32),
        compiler_params=_cp(("arbitrary",)), name="dsa_index_sample")(page_table, iq, iw, pool_t, new_rows)
    return out.reshape(ns, 1, wide)


def _diff_post_kernel(o_ref, lam_ref, g_ref, out_ref, *, lam_init):
    lv = lam_ref[...]
    lam = (jnp.exp(jnp.sum(lv[0:1] * lv[1:2], axis=-1, keepdims=True))
           - jnp.exp(jnp.sum(lv[2:3] * lv[3:4], axis=-1, keepdims=True)) + lam_init)
    o = o_ref[...]
    d = o[:, 0:N_HEADS, :] - lam * o[:, N_HEADS:2 * N_HEADS, :]
    d = d * lax.rsqrt(jnp.mean(d * d, axis=-1, keepdims=True) + LN_EPS)
    out_ref[...] = d * g_ref[...] * (1.0 - lam_init)


def _diff_post(o, lam, g, lam_init):
    ns = o.shape[0]
    vm = pl.BlockSpec(memory_space=pltpu.VMEM)
    return pl.pallas_call(
        functools.partial(_diff_post_kernel, lam_init=lam_init), in_specs=[vm, vm, vm], out_specs=vm,
        out_shape=jax.ShapeDtypeStruct((ns, N_HEADS, LANES), F32),
        compiler_params=_cp(None), name="diff_post_sample")(o, lam, g.reshape(1, LANES))


def _rope_group(x, cos, sin):
    lane = lax.broadcasted_iota(I32, x.shape, 1)
    half = MLA_ROPE // 2
    other = jnp.where(lane < half, pltpu.roll(x, LANES - half, 1), pltpu.roll(x, half, 1))
    return x * cos + other * sin


def _mla_prep_kernel(h_ref, qg_ref, kg_ref, cos_ref, sin_ref, qn_ref, kv_ref):
    qa = h_ref[:, 0:MLA_Q_RANK]
    qn = qa * lax.rsqrt(jnp.mean(qa * qa, axis=-1, keepdims=True) + LN_EPS) * qg_ref[...]
    qn_ref[...] = qn.astype(BF16)
    kva = h_ref[:, MLA_Q_RANK:MLA_Q_RANK + MLA_KV_RANK]
    kv_ref[:, 0:MLA_KV_RANK] = kva * lax.rsqrt(jnp.mean(kva * kva, axis=-1, keepdims=True) + LN_EPS) * kg_ref[...]
    kr = h_ref[:, MLA_Q_RANK + MLA_KV_RANK + LANES:MLA_Q_RANK + MLA_KV_RANK + 2 * LANES]
    kv_ref[:, MLA_KV_RANK:MLA_KV_RANK + LANES] = _rope_group(kr, cos_ref[...], sin_ref[...])


def _mla_prep(h, qg, kg, cos, sin):
    m = h.shape[0]
    tm = _tile(m, 640, 16)
    wide = MLA_Q_RANK + MLA_KV_RANK + 2 * LANES
    return pl.pallas_call(
        _mla_prep_kernel, grid=(m // tm,),
        in_specs=[pl.BlockSpec((tm, wide), lambda i: (i, 1)),
                  pl.BlockSpec((1, MLA_Q_RANK), lambda i: (0, 0)), pl.BlockSpec((1, MLA_KV_RANK), lambda i: (0, 0)),
                  pl.BlockSpec((tm, LANES), lambda i: (i, 0)), pl.BlockSpec((tm, LANES), lambda i: (i, 0))],
        out_specs=[pl.BlockSpec((tm, MLA_Q_RANK), lambda i: (i, 0)),
                   pl.BlockSpec((tm, MLA_KV_RANK + LANES), lambda i: (i, 0))],
        out_shape=[jax.ShapeDtypeStruct((m, MLA_Q_RANK), BF16),
                   jax.ShapeDtypeStruct((m, MLA_KV_RANK + LANES), F32)],
        compiler_params=_cp(("parallel",)), name="mla_prep")(
            h, qg.reshape(1, -1), kg.reshape(1, -1), cos, sin)


def _mla_q_kernel(qf_ref, wuk_ref, cos_ref, sin_ref, o_ref):
    width = MLA_KV_RANK + LANES
    for h in range(N_HEADS):
        nope = qf_ref[:, h * LANES:(h + 1) * LANES].astype(BF16)
        o_ref[:, h * width:h * width + MLA_KV_RANK] = jnp.dot(
            nope, wuk_ref[h], preferred_element_type=F32).astype(o_ref.dtype)
        rope = qf_ref[:, (N_HEADS + h) * LANES:(N_HEADS + h + 1) * LANES]
        o_ref[:, h * width + MLA_KV_RANK:(h + 1) * width] = _rope_group(
            rope, cos_ref[...], sin_ref[...]).astype(o_ref.dtype)


def _mla_q(qf, wuk_t, cos, sin, out_dtype):
    m = qf.shape[0]
    tm = _tile(m, 640, 16)
    width = N_HEADS * (MLA_KV_RANK + LANES)
    return pl.pallas_call(
        _mla_q_kernel, grid=(m // tm,),
        in_specs=[pl.BlockSpec((tm, qf.shape[1]), lambda i: (i, 0)),
                  pl.BlockSpec(wuk_t.shape, lambda i: (0, 0, 0)),
                  pl.BlockSpec((tm, LANES), lambda i: (i, 0)), pl.BlockSpec((tm, LANES), lambda i: (i, 0))],
        out_specs=pl.BlockSpec((tm, width), lambda i: (i, 0)),
        out_shape=jax.ShapeDtypeStruct((m, width), out_dtype),
        compiler_params=_cp(("parallel",)), name="mla_q")(qf, wuk_t, cos, sin)


def _mla_uv_kernel(o_ref, w_ref, d_ref):
    for h in range(N_HEADS):
        d_ref[:, h * MLA_VDH:(h + 1) * MLA_VDH] = jnp.dot(
            o_ref[:, h * MLA_KV_RANK:(h + 1) * MLA_KV_RANK], w_ref[h], preferred_element_type=F32).astype(BF16)


def _mla_uv(o_lat, wuv):
    m = o_lat.shape[0]
    tm = _tile(m, 640, 16)
    return pl.pallas_call(
        _mla_uv_kernel, grid=(m // tm,),
        in_specs=[pl.BlockSpec((tm, o_lat.shape[1]), lambda i: (i, 0)), pl.BlockSpec(wuv.shape, lambda i: (0, 0, 0))],
        out_specs=pl.BlockSpec((tm, N_HEADS * MLA_VDH), lambda i: (i, 0)),
        out_shape=jax.ShapeDtypeStruct((m, N_HEADS * MLA_VDH), BF16),
        compiler_params=_cp(("parallel",)), name="mla_uv")(o_lat, wuv)


def _prep_w_in_even(w):
    d = w.shape[0]
    pad = jnp.zeros((d, 3200 - 3144), w.dtype)
    cols = [w[:, 0:1024], w[:, 1280:2304], w[:, 2560:3072], w[:, 1024:1280], w[:, 2304:2560], w[:, 3072:3144], pad]
    return jnp.concatenate(cols, axis=1).astype(BF16)


def _prep_w_in_odd(w):
    d = w.shape[0]
    mq = jnp.pad(w[:, 0:512].reshape(d, N_HEADS, MOBA_DH), ((0, 0), (0, 0), (0, LANES - MOBA_DH))).reshape(d, -1)
    kr = jnp.pad(w[:, 1408:1472], ((0, 0), (0, LANES - MLA_ROPE)))
    return jnp.concatenate([mq, w[:, 640:1152], w[:, 1152:1408], w[:, 512:640], kr], axis=1).astype(BF16)


def _prep_w_q_up(w):
    r = w.shape[0]
    w3 = w.reshape(r, N_HEADS, MLA_NOPE + MLA_ROPE)
    nope = w3[:, :, :MLA_NOPE].reshape(r, -1)
    rope = jnp.pad(w3[:, :, MLA_NOPE:], ((0, 0), (0, 0), (0, LANES - MLA_ROPE))).reshape(r, -1)
    return jnp.concatenate([nope, rope], axis=1).astype(BF16)


def _rope_tables(pos):
    half = MLA_ROPE // 2
    freq = jnp.power(ROPE_THETA, -jnp.arange(half, dtype=F32) / half)
    ang = pos.astype(F32)[:, None] * freq
    cos, sin = jnp.cos(ang), jnp.sin(ang)
    zero = jnp.zeros((pos.shape[0], LANES - MLA_ROPE), F32)
    return jnp.concatenate([cos, cos, zero], axis=1), jnp.concatenate([-sin, sin, zero], axis=1)


def _sample_rows(x, n_rows=SROWS):
    return jnp.pad(x, ((0, 0), (0, n_rows - x.shape[1]), (0, 0)))


def kernel(x_prompt, x_sample, cache_diff_kv, cache_dsa_kv, cache_dsa_idx, cache_moba_kv, cache_mla, page_table, p_prompt, p_sample, rel_bias, w_in_even, diff_lambda, diff_subln_g, w_out_even, ffn_w_in, ffn_w_out, w_in_odd, mla_q_norm_g, mla_kv_norm_g, mla_w_q_up, mla_w_uk, mla_w_uv, w_out_odd, moe_router, moe_w_in, moe_w_out, ln1_g, ln1_b, ln2_g, ln2_b, ple_w_gate, ple_w_proj):
    b, l, d = x_prompt.shape
    ns, dec_seq, _ = x_sample.shape
    depth = ln1_g.shape[0]
    n_pages = page_table.shape[1]
    past = n_pages * PAGE
    n_prompt = b * l
    assert dec_seq == 1 and past % MOBA_BLOCK == 0 and n_pages % CHUNK_PAGES == 0
    assert l % TK == 0 and TK == MOBA_BLOCK and TK == 2 * TQ
    alpha = (2 * depth) ** 0.25

    x = jnp.concatenate([x_prompt.reshape(n_prompt, d), x_sample.reshape(ns, d)], axis=0)
    xb = x.astype(BF16)
    ple = jnp.concatenate([p_prompt.reshape(depth, n_prompt, -1), p_sample.reshape(depth, ns, -1)], axis=1).astype(BF16)

    tbl8 = _bias_tables_prompt(rel_bias)
    tbl0 = _bias_tables_prompt(jnp.zeros((REL_BUCKETS, 1), F32))
    tbs = _bias_table_sample(rel_bias, past)
    pos = jnp.concatenate([jnp.tile(jnp.arange(l), b), jnp.full((ns,), past)])
    cos, sin = _rope_tables(pos)
    lane = jnp.arange(LANES)
    idx_pool_t = jnp.swapaxes(cache_dsa_idx, 2, 3)
    mla_pool_t = jnp.swapaxes(cache_mla, 2, 3)
    n_tiles = (2 * (n_prompt + ns) + N_EXPERTS * (MOE_TILE - 1)) // MOE_TILE
    moe_w_out_b = moe_w_out.astype(BF16)

    rows_diff, rows_dsa, rows_idx, rows_moba, rows_mla = [], [], [], [], []
    for i in range(depth):
        j = i // 2
        if i % 2 == 0:
            lam_init = 0.8 - 0.6 * math.exp(-0.3 * i)
            h = _matmul(xb, _prep_w_in_even(w_in_even[j]), F32, "in_proj_even", tn_pref=640)
            r_diff, r_dsa, r_idx = h[:, 2560:2816], h[:, 2816:3072], h[:, 3072:3072 + IDX_DH]
            rows_diff.append(r_diff); rows_dsa.append(r_dsa); rows_idx.append(r_idx)
            hp = h[:n_prompt].reshape(b, l, -1)
            hs = h[n_prompt:]
            a_p = _flash_t("A", jnp.swapaxes(hp[:, :, 0:1024], 1, 2), hp, jnp.swapaxes(hp[:, :, 2688:2816], 1, 2),
                           tbl8, k_blk=20, k_width=LANES, out_width=1024, groups=2 * N_HEADS,
                           scale=DIFF_DH ** -0.5, lam=diff_lambda[j], subln_g=diff_subln_g[j], lam_init=lam_init,
                           name="diff_attn_prompt")
            topk = min(DSA_TOPK, l // 4)
            sel_t = _idx_prompt(jnp.swapaxes(hp[:, :, 2048:2560], 1, 2), hp,
                                jnp.swapaxes(hp[:, :, 3136:3136 + IDX_HEADS], 1, 2), topk)
            s_p = _flash_t("B", jnp.swapaxes(hp[:, :, 1024:2048], 1, 2), hp, jnp.swapaxes(hp[:, :, 2944:3072], 1, 2),
                           tbl8, k_blk=22, k_width=LANES, out_width=1024, groups=N_HEADS, scale=DSA_DH ** -0.5,
                           mask_t=sel_t, name="dsa_attn_prompt")
            dq = hs[:, 0:1024].reshape(ns, N_HEADS, LANES)
            q_a = jnp.concatenate([jnp.where(lane < DIFF_DH, dq, 0.0), jnp.where(lane >= DIFF_DH, dq, 0.0)], axis=1)
            o_a = _sample_attn("A", q_a, cache_diff_kv, j, page_table, hs[:, None, 2560:2816], tbs,
                               pieces=((0, LANES),), v_lo=LANES, v_hi=2 * LANES, scale=DIFF_DH ** -0.5,
                               name="diff_attn_sample")
            a_s = _diff_post(o_a, diff_lambda[j], diff_subln_g[j], lam_init).reshape(ns, -1)
            iq = _sample_rows(hs[:, 2048:2560].reshape(ns, IDX_HEADS, IDX_DH))
            iw = _sample_rows(jnp.broadcast_to(hs[:, 3072 + IDX_DH:3072 + IDX_DH + IDX_HEADS, None],
                                               (ns, IDX_HEADS, LANES)))
            topk_s = min(DSA_TOPK, (past + 1) // 4)
            sel_s = _idx_sample(iq, iw, idx_pool_t, j, page_table, hs[:, None, 3072:3072 + IDX_DH], topk_s)
            q_s = _sample_rows(hs[:, 1024:2048].reshape(ns, N_HEADS, LANES))
            o_s = _sample_attn("B", q_s, cache_dsa_kv, j, page_table, hs[:, None, 2816:3072], tbs,
                               pieces=((0, LANES),), v_lo=LANES, v_hi=2 * LANES, scale=DSA_DH ** -0.5,
                               extra=sel_s, name="dsa_attn_sample")
            s_s = o_s[:, :N_HEADS].reshape(ns, -1)
            mix_p = jnp.concatenate([a_p.reshape(n_prompt, -1), s_p.reshape(n_prompt, -1)], axis=1)
            mix_s = jnp.concatenate([a_s, s_s], axis=1).astype(BF16)
            mix = jnp.concatenate([mix_p, mix_s], axis=0)
            hout = _matmul(mix, w_out_even[j].astype(BF16), F32, "out_proj_even")
        else:
            h = _matmul(xb, _prep_w_in_odd(w_in_odd[j]), F32, "in_proj_odd")
            r_moba = h[:, 1792:1920]
            rows_moba.append(r_moba)
            qn, kvrow = _mla_prep(h, mla_q_norm_g[j], mla_kv_norm_g[j], cos, sin)
            rows_mla.append(kvrow[:, :MLA_KV_RANK + MLA_ROPE])
            qf = _matmul(qn, _prep_w_q_up(mla_w_q_up[j]), F32, "mla_q_up")
            wuk_t = jnp.transpose(mla_w_uk[j], (1, 2, 0)).astype(BF16)
            wuv = jnp.transpose(mla_w_uv[j], (1, 0, 2)).astype(BF16)
            qmla = _mla_q(qf, wuk_t, cos, sin, F32)
            hp = h[:n_prompt].reshape(b, l, -1)
            hs = h[n_prompt:]
            m_p = _flash_t("C", jnp.swapaxes(hp[:, :, 0:1024], 1, 2), hp, jnp.swapaxes(hp[:, :, 1792:1920], 1, 2),
                           tbl8, k_blk=14, k_width=LANES, out_width=N_HEADS * MOBA_DH, groups=N_HEADS,
                           scale=MOBA_DH ** -0.5, n_sel=min(MOBA_TOPK, l // MOBA_BLOCK), name="moba_attn_prompt")
            width = MLA_KV_RANK + LANES
            kv_p = kvrow[:n_prompt].reshape(b, l, -1)
            o_lat_p = _flash_t("D", jnp.swapaxes(qmla[:n_prompt].reshape(b, l, -1), 1, 2), kv_p,
                               jnp.swapaxes(kv_p[:, :, :MLA_KV_RANK], 1, 2), tbl0, k_blk=0, k_width=width,
                               out_width=N_HEADS * MLA_KV_RANK, groups=N_HEADS,
                               scale=(MLA_NOPE + MLA_ROPE) ** -0.5, name="mla_attn_prompt")
            q_c = _sample_rows(hs[:, 0:1024].reshape(ns, N_HEADS, LANES))
            o_c = _sample_attn("C", q_c, cache_moba_kv, j, page_table, hs[:, None, 1792:1920], tbs,
                               pieces=((0, LANES),), v_lo=0, v_hi=LANES, scale=MOBA_DH ** -0.5,
                               n_sel=min(MOBA_TOPK, (past + 1) // MOBA_BLOCK), name="moba_attn_sample")
            m_s = o_c[:, :N_HEADS, MOBA_DH:].reshape(ns, -1)
            q_d = _sample_rows(qmla[n_prompt:].reshape(ns, N_HEADS, width)[:, :, :MLA_KV_RANK + MLA_ROPE])
            o_d = _sample_attn("D", q_d, mla_pool_t, j, page_table, kvrow[n_prompt:, None, :MLA_KV_RANK + MLA_ROPE],
                               None, pieces=((0, MLA_KV_RANK), (MLA_KV_RANK, MLA_KV_RANK + MLA_ROPE)),
                               v_lo=0, v_hi=MLA_KV_RANK, scale=(MLA_NOPE + MLA_ROPE) ** -0.5, transposed=True,
                               name="mla_attn_sample")
            o_lat = jnp.concatenate([o_lat_p.reshape(n_prompt, -1),
                                     o_d[:, :N_HEADS].reshape(ns, -1).astype(BF16)], axis=0)
            dmla = _mla_uv(o_lat, wuv)
            m_all = jnp.concatenate([m_p.reshape(n_prompt, -1), m_s.astype(BF16)], axis=0)
            mix = jnp.concatenate([m_all, dmla], axis=1)
            hout = _matmul(mix, w_out_odd[j].astype(BF16), F32, "out_proj_odd")

        x, xb = _deepnorm_ln(x, hout, ln1_g[i], ln1_b[i], alpha, "deepnorm_ln1")
        if i % 2 == 0:
            mid = _swiglu(xb, ffn_w_in[j][None].astype(BF16), "ffn_in")
            f = _matmul(mid, ffn_w_out[j].astype(BF16), F32, "ffn_out")
            x, xb = _deepnorm_ln(x, f, ln2_g[i], ln2_b[i], alpha, "deepnorm_ln2")
        else:
            router = jnp.pad(moe_router[j], ((0, 0), (0, LANES - N_EXPERTS))).astype(BF16)
            route = _router(xb, router, "moe_router")
            slot1, slot2, token_of_slot, tile_expert, n_used = _dispatch_plan(route, n_tiles)
            xs = _gather_rows(x, token_of_slot)
            mid = _grouped_swiglu(xs, moe_w_in, j, tile_expert, n_used)
            ys = _grouped_matmul(mid, moe_w_out_b, j, tile_expert, n_used)
            x, xb = _moe_combine_ln(ys, slot1, slot2, route, x, ln2_g[i], ln2_b[i], alpha)
        x, xb = _ple(xb, x, ple_w_gate[i].astype(BF16), ple[i], ple_w_proj[i].astype(BF16), "ple_gate")

    def split(rows, width):
        r = jnp.stack(rows, axis=0)
        rp = jnp.transpose(r[:, :n_prompt].reshape(-1, b, l, width), (1, 0, 2, 3))
        rs = jnp.transpose(r[:, n_prompt:].reshape(-1, ns, 1, width), (1, 0, 2, 3))
        return rp, rs

    dp, ds = split(rows_diff, 256)
    sp, ss = split(rows_dsa, 256)
    ip, is_ = split(rows_idx, IDX_DH)
    mp, ms = split(rows_moba, 2 * MOBA_DH)
    lp, ls = split(rows_mla, MLA_KV_RANK + MLA_ROPE)
    return (x[:n_prompt].reshape(b, l, d), x[n_prompt:].reshape(ns, 1, d), dp, ds, sp, ss, ip, is_, mp, ms, lp, ls)
```
